```python
import jax, jax.numpy as jnp
from jax import lax
import numpy as np

D_MODEL = 1024
BATCH = 16
SEQ = 4096
DEPTH = 4

D_MIX = D_MODEL
RET_HEADS = 4
RET_HEAD_DIM = 128
RET_WIDTH = RET_HEADS * RET_HEAD_DIM
RET_CHUNK = 128
GDN_HEADS = 4
GDN_HEAD_DIM = 128
GDN_WIDTH = GDN_HEADS * GDN_HEAD_DIM
GDN_CHUNK = 64
CONV_K = 5
D_FF = 2816
MEM_LEN = 256
XATTN_HEADS = 4
XATTN_HEAD_DIM = D_MODEL // XATTN_HEADS
ROPE_BASE = 10000.0
NORM_EPS = 1e-6
N_SUBLAYERS = 4
IN_COLS = 4 * RET_WIDTH + 4 * GDN_WIDTH + 4 * GDN_HEADS
IN_SPLITS = (RET_WIDTH, 2 * RET_WIDTH, 3 * RET_WIDTH, 4 * RET_WIDTH,
             4 * RET_WIDTH + 3 * GDN_WIDTH, 4 * RET_WIDTH + 4 * GDN_WIDTH,
             4 * RET_WIDTH + 4 * GDN_WIDTH + 2 * GDN_HEADS)

kernel_name = "hybrid_retention_gdn_macaron_encoder"


def rms_norm(x, w):
    xf = x.astype(jnp.float32)
    y = xf * lax.rsqrt(jnp.mean(xf * xf, axis=-1, keepdims=True) + NORM_EPS)
    return (y * w.astype(jnp.float32)).astype(x.dtype)


def swiglu(u, w_gate, w_up, w_down):
    return (jax.nn.silu(u @ w_gate) * (u @ w_up)) @ w_down


def rotary_tables(positions):
    inv_freq = ROPE_BASE ** (-jnp.arange(0, RET_HEAD_DIM, 2, dtype=jnp.float32) / RET_HEAD_DIM)
    ang = positions.astype(jnp.float32)[..., None] * inv_freq
    return jnp.cos(ang)[:, :, None, :], jnp.sin(ang)[:, :, None, :]


def rotary(x, cos, sin):
    half = x.shape[-1] // 2
    x1, x2 = x[..., :half], x[..., half:]
    return jnp.concatenate([x1 * cos - x2 * sin, x1 * sin + x2 * cos], axis=-1)


def flip_seq(t):
    return jnp.flip(t, axis=2)


def retention_direction(q, k, v, log_gamma, strict):
    b, h, s, d = q.shape
    n = s // RET_CHUNK
    q, k, v = (t.reshape(b, h, n, RET_CHUNK, d) for t in (q, k, v))
    pos = jnp.arange(RET_CHUNK, dtype=jnp.float32)
    rel = pos[:, None] - pos[None, :]
    mask = rel > 0 if strict else rel >= 0
    lg = log_gamma[:, None, None]
    dmat = jnp.where(mask, jnp.exp(jnp.where(mask, rel, 0.0) * lg), 0.0)
    scores = jnp.einsum('bhnid,bhnjd->bhnij', q, k) * dmat[:, None]
    o_intra = jnp.einsum('bhnij,bhnje->bhnie', scores, v)
    lgv = log_gamma[:, None]
    k_dec = k * jnp.exp((RET_CHUNK - 1.0 - pos) * lgv)[:, None, :, None]
    chunk_kv = jnp.einsum('bhncd,bhnce->bhnde', k_dec, v)
    chunk_decay = jnp.exp(RET_CHUNK * log_gamma)[None, :, None, None]

    def step(state, kv_c):
        return state * chunk_decay + kv_c, state

    _, prev = lax.scan(step, jnp.zeros((b, h, d, d), q.dtype), jnp.moveaxis(chunk_kv, 2, 0))
    prev = jnp.moveaxis(prev, 0, 2)
    q_dec = q * jnp.exp((pos + 1.0) * lgv)[:, None, :, None]
    o_inter = jnp.einsum('bhncd,bhnde->bhnce', q_dec, prev)
    return (o_intra + o_inter).reshape(b, h, s, d)


def retention_group(rq, rk, rv, rg, cos, sin, log_gamma):
    b, s, _ = rq.shape
    heads = lambda t: t.reshape(b, s, RET_HEADS, RET_HEAD_DIM)
    q = rotary(heads(rq), cos, sin)
    k = rotary(heads(rk), cos, sin) * RET_HEAD_DIM ** -0.5
    v = heads(rv)
    q, k, v = (jnp.swapaxes(t, 1, 2).astype(jnp.float32) for t in (q, k, v))
    lgam = log_gamma.astype(jnp.float32)
    fwd = retention_direction(q, k, v, lgam[0], strict=False)
    bwd = retention_direction(flip_seq(q), flip_seq(k), flip_seq(v), lgam[1], strict=True)
    o = fwd + flip_seq(bwd)
    mu = jnp.mean(o, axis=-1, keepdims=True)
    var = jnp.mean(jnp.square(o - mu), axis=-1, keepdims=True)
    o = (o - mu) * lax.rsqrt(var + NORM_EPS)
    o = jnp.swapaxes(o, 1, 2).reshape(b, s, RET_WIDTH).astype(rg.dtype)
    return jax.nn.silu(rg) * o


def gdn_direction(q, k, v, g, beta):
    b, h, s, dk = q.shape
    dv = v.shape[-1]
    n = s // GDN_CHUNK
    c = GDN_CHUNK
    q, k, v = (t.reshape(b, h, n, c, t.shape[-1]) for t in (q, k, v))
    g = jnp.cumsum(g.reshape(b, h, n, c), axis=-1)
    beta = beta.reshape(b, h, n, c)
    tril = jnp.tril(jnp.ones((c, c), bool))
    strict = jnp.tril(jnp.ones((c, c), bool), -1)
    decay = jnp.exp(jnp.where(tril, g[..., :, None] - g[..., None, :], -jnp.inf))
    k_beta = k * beta[..., None]
    lower = jnp.where(strict, jnp.einsum('bhnid,bhnjd->bhnij', k_beta, k) * decay, 0.0)
    rhs = jnp.concatenate([v * beta[..., None], k_beta * jnp.exp(g)[..., None]], axis=-1)
    sol = lax.linalg.triangular_solve(lower, rhs, left_side=True, lower=True, unit_diagonal=True)
    u, w = sol[..., :dv], sol[..., dv:]
    qk = jnp.einsum('bhnid,bhnjd->bhnij', q, k) * decay
    q_g = q * jnp.exp(g)[..., None]
    k_tail = k * jnp.exp(g[..., -1:] - g)[..., None]
    g_last = jnp.exp(g[..., -1])

    def step(state, xs):
        u_c, w_c, qk_c, qg_c, kt_c, gl_c = xs
        v_new = u_c - w_c @ state
        o_c = qg_c @ state + qk_c @ v_new
        state = state * gl_c[..., None, None] + jnp.swapaxes(kt_c, -1, -2) @ v_new
        return state, o_c

    xs = tuple(jnp.moveaxis(t, 2, 0) for t in (u, w, qk, q_g, k_tail, g_last))
    _, o = lax.scan(step, jnp.zeros((b, h, dk, dv), jnp.float32), xs)
    return jnp.moveaxis(o, 0, 2).reshape(b, h, s, dv)


def centred_depthwise_conv(x, w):
    return lax.conv_general_dilated(
        x, w[:, None, :], window_strides=(1,),
        padding=[((CONV_K - 1) // 2, CONV_K // 2)],
        dimension_numbers=('NWC', 'WIO', 'NWC'),
        feature_group_count=x.shape[-1])


def gdn_group(qkv, z, a, bl, conv_w, a_log, dt_bias, norm_w):
    b, s, _ = qkv.shape
    qkv = jax.nn.silu(centred_depthwise_conv(qkv, conv_w))
    q, k, v = jnp.split(qkv, 3, axis=-1)
    heads = lambda t: jnp.swapaxes(t.reshape(b, s, GDN_HEADS, GDN_HEAD_DIM), 1, 2).astype(jnp.float32)
    q, k, v = heads(q), heads(k), heads(v)
    l2n = lambda t: t * lax.rsqrt(jnp.sum(t * t, axis=-1, keepdims=True) + NORM_EPS)
    q = l2n(q) * GDN_HEAD_DIM ** -0.5
    k = l2n(k)
    dir_heads = lambda t: jnp.transpose(t.reshape(b, s, 2, GDN_HEADS), (2, 0, 3, 1)).astype(jnp.float32)
    a, bl = dir_heads(a), dir_heads(bl)
    g = -jnp.exp(a_log.astype(jnp.float32))[:, None, :, None] * jax.nn.softplus(
        a + dt_bias.astype(jnp.float32)[:, None, :, None])
    beta = jax.nn.sigmoid(bl)
    fwd = gdn_direction(q, k, v, g[0], beta[0])
    bwd = gdn_direction(flip_seq(q), flip_seq(k), flip_seq(v), flip_seq(g[1]), flip_seq(beta[1]))
    o = fwd + flip_seq(bwd)
    o = o * lax.rsqrt(jnp.mean(o * o, axis=-1, keepdims=True) + NORM_EPS) * norm_w.astype(jnp.float32)
    o = jnp.swapaxes(o, 1, 2).reshape(b, s, GDN_WIDTH).astype(z.dtype)
    return o * jax.nn.silu(z)


def parallel_head_groups(u, cos, sin, w_in, conv_w, ret_log_gamma, a_log, dt_bias, gdn_norm_w, w_out):
    proj = u @ w_in
    rq, rk, rv, rg, qkv, z, a, bl = jnp.split(proj, IN_SPLITS, axis=-1)
    o_ret = retention_group(rq, rk, rv, rg, cos, sin, ret_log_gamma)
    o_gdn = gdn_group(qkv, z, a, bl, conv_w, a_log, dt_bias, gdn_norm_w)
    return jnp.concatenate([o_ret, o_gdn], axis=-1) @ w_out


def memory_cross_attention(u, mem_n, wq, wk, wv, wo):
    b, s, _ = u.shape
    m = mem_n.shape[1]
    q = (u @ wq).reshape(b, s, XATTN_HEADS, XATTN_HEAD_DIM)
    k = (mem_n @ wk).reshape(b, m, XATTN_HEADS, XATTN_HEAD_DIM)
    v = (mem_n @ wv).reshape(b, m, XATTN_HEADS, XATTN_HEAD_DIM)
    scores = jnp.einsum('bshd,bmhd->bhsm', q, k).astype(jnp.float32) * XATTN_HEAD_DIM ** -0.5
    p = jax.nn.softmax(scores, axis=-1).astype(v.dtype)
    o = jnp.einsum('bhsm,bmhd->bshd', p, v).reshape(b, s, D_MODEL)
    return o @ wo


def setup_inputs(seed: int = 0) -> dict:
    key = jax.random.key(seed)
    ks = jax.random.split(key, 24)
    nrm = lambda k, shape: jax.random.normal(k, shape, jnp.float32)
    dense = lambda k, shape, fan_in: nrm(k, shape) * fan_in ** -0.5
    gain = lambda k, shape: 1.0 + 0.02 * nrm(k, shape)
    x = nrm(ks[0], (BATCH, SEQ, D_MODEL))
    mem = nrm(ks[1], (BATCH, MEM_LEN, D_MODEL))
    positions = jnp.broadcast_to(jnp.arange(SEQ, dtype=jnp.int32)[None, :], (BATCH, SEQ))
    base = jnp.log1p(-(2.0 ** (-5.0 - jnp.arange(RET_HEADS, dtype=jnp.float32))))
    ret_log_gamma = base * (1.0 + 0.05 * nrm(ks[10], (DEPTH, 2, RET_HEADS)))
    gdn_a_log = jnp.log(jax.random.uniform(ks[11], (DEPTH, 2, GDN_HEADS), jnp.float32, 1.0, 16.0))
    dt = jnp.exp(jax.random.uniform(ks[12], (DEPTH, 2, GDN_HEADS), jnp.float32,
                                    float(np.log(1e-3)), float(np.log(1e-1))))
    gdn_dt_bias = dt + jnp.log(-jnp.expm1(-dt))
    return {
        "x": x,
        "mem": mem,
        "positions": positions,
        "norm_pre": gain(ks[2], (DEPTH, N_SUBLAYERS, D_MODEL)),
        "norm_post": gain(ks[3], (DEPTH, N_SUBLAYERS, D_MODEL)),
        "mem_norm": gain(ks[4], (DEPTH, D_MODEL)),
        "ffn1_gate": dense(ks[5], (DEPTH, D_MODEL, D_FF), D_MODEL),
        "ffn1_up": dense(ks[6], (DEPTH, D_MODEL, D_FF), D_MODEL),
        "ffn1_down": dense(ks[7], (DEPTH, D_FF, D_MODEL), D_FF),
        "w_in": dense(ks[8], (DEPTH, D_MODEL, IN_COLS), D_MODEL),
        "gdn_conv": dense(ks[9], (DEPTH, CONV_K, 3 * GDN_WIDTH), CONV_K),
        "ret_log_gamma": ret_log_gamma,
        "gdn_a_log": gdn_a_log,
        "gdn_dt_bias": gdn_dt_bias,
        "gdn_norm": gain(ks[13], (DEPTH, GDN_HEAD_DIM)),
        "w_out": dense(ks[14], (DEPTH, D_MIX, D_MODEL), D_MIX),
        "xattn_q": dense(ks[15], (DEPTH, D_MODEL, D_MODEL), D_MODEL),
        "xattn_k": dense(ks[16], (DEPTH, D_MODEL, D_MODEL), D_MODEL),
        "xattn_v": dense(ks[17], (DEPTH, D_MODEL, D_MODEL), D_MODEL),
        "xattn_o": dense(ks[18], (DEPTH, D_MODEL, D_MODEL), D_MODEL),
        "ffn2_gate": dense(ks[19], (DEPTH, D_MODEL, D_FF), D_MODEL),
        "ffn2_up": dense(ks[20], (DEPTH, D_MODEL, D_FF), D_MODEL),
        "ffn2_down": dense(ks[21], (DEPTH, D_FF, D_MODEL), D_FF),
    }


def reference(x, mem, positions, norm_pre, norm_post, mem_norm, ffn1_gate, ffn1_up, ffn1_down,
              w_in, gdn_conv, ret_log_gamma, gdn_a_log, gdn_dt_bias, gdn_norm, w_out,
              xattn_q, xattn_k, xattn_v, xattn_o, ffn2_gate, ffn2_up, ffn2_down):
    cos, sin = rotary_tables(positions)
    for l in range(DEPTH):
        h = swiglu(rms_norm(x, norm_pre[l, 0]), ffn1_gate[l], ffn1_up[l], ffn1_down[l])
        x = x + 0.5 * rms_norm(h, norm_post[l, 0])
        h = parallel_head_groups(rms_norm(x, norm_pre[l, 1]), cos, sin, w_in[l], gdn_conv[l],
                                 ret_log_gamma[l], gdn_a_log[l], gdn_dt_bias[l], gdn_norm[l], w_out[l])
        x = x + rms_norm(h, norm_post[l, 1])
        h = memory_cross_attention(rms_norm(x, norm_pre[l, 2]), rms_norm(mem, mem_norm[l]),
                                   xattn_q[l], xattn_k[l], xattn_v[l], xattn_o[l])
        x = x + rms_norm(h, norm_post[l, 2])
        h = swiglu(rms_norm(x, norm_pre[l, 3]), ffn2_gate[l], ffn2_up[l], ffn2_down[l])
        x = x + 0.5 * rms_norm(h, norm_post[l, 3])
    return x
```

```python
import functools

import jax
import jax.numpy as jnp
from jax import lax
from jax.experimental import pallas as pl
from jax.experimental.pallas import tpu as pltpu

F32 = jnp.float32
BF16 = jnp.bfloat16

HEAD_DIM = 128
RET_HEADS = 4
RET_CHUNK = 128
GDN_HEADS = 4
GDN_CHUNK = 64
GDN_SUPER = 256
CONV_K = 5
CONV_HALO = 8
XATTN_HEADS = 4
ROPE_BASE = 10000.0
NORM_EPS = 1e-6
LANES = 128
VMEM_LIMIT = 56 * 1024 * 1024


def _cparams(*sem):
    return pltpu.CompilerParams(dimension_semantics=sem, vmem_limit_bytes=VMEM_LIMIT)


def _rms(x, w):
    return x * lax.rsqrt(jnp.mean(x * x, axis=-1, keepdims=True) + NORM_EPS) * w


def _silu(x):
    return x * jax.nn.sigmoid(x)


def _dot(a, b):
    return jnp.dot(a.astype(BF16), b.astype(BF16), preferred_element_type=F32)


def _dot_nt(a, b):
    return lax.dot_general(a.astype(BF16), b.astype(BF16), (((1,), (1,)), ((), ())),
                           preferred_element_type=F32)


def _dot_tn(a, b):
    return lax.dot_general(a.astype(BF16), b.astype(BF16), (((0,), (0,)), ((), ())),
                           preferred_element_type=F32)


def _split2(a):
    hi = a.astype(BF16)
    lo = (a - hi.astype(F32)).astype(BF16)
    return hi, lo


def _split3(a):
    hi = a.astype(BF16)
    r = a - hi.astype(F32)
    mid = r.astype(BF16)
    lo = (r - mid.astype(F32)).astype(BF16)
    return hi, mid, lo


def _dot_split(a, b):
    ah, al = _split2(a)
    bh, bl = _split2(b)
    d = functools.partial(jnp.dot, preferred_element_type=F32)
    return d(ah, bh) + (d(al, bh) + d(ah, bl))


def _rope_kernel(pos_ref, invf_ref, sign_ref, cos_ref, sin_ref):
    ang = pos_ref[0].astype(F32) * invf_ref[...]
    cos_ref[0] = jnp.cos(ang)
    sin_ref[0] = jnp.sin(ang) * sign_ref[...]


def _rope_tables(positions):
    b, s = positions.shape
    ts = min(s, 512)
    half = HEAD_DIM // 2
    inv_freq = ROPE_BASE ** (-jnp.arange(0, HEAD_DIM, 2, dtype=F32) / HEAD_DIM)
    invf = jnp.concatenate([inv_freq, inv_freq])[None, :]
    sign = jnp.concatenate([-jnp.ones((half,), F32), jnp.ones((half,), F32)])[None, :]
    tab = jax.ShapeDtypeStruct((b, s, HEAD_DIM), F32)
    return pl.pallas_call(
        _rope_kernel,
        grid=(b, s // ts),
        in_specs=[pl.BlockSpec((1, ts, 1), lambda i, j: (i, j, 0)),
                  pl.BlockSpec((1, HEAD_DIM), lambda i, j: (0, 0)),
                  pl.BlockSpec((1, HEAD_DIM), lambda i, j: (0, 0))],
        out_specs=[pl.BlockSpec((1, ts, HEAD_DIM), lambda i, j: (i, j, 0)),
                   pl.BlockSpec((1, ts, HEAD_DIM), lambda i, j: (i, j, 0))],
        out_shape=[tab, tab],
        compiler_params=_cparams("parallel", "parallel"),
        name="rope_tables",
    )(positions.reshape(b, s, 1), invf, sign)


def _ffn_kernel(x_ref, npre_ref, wg_ref, wu_ref, wd_ref, npost_ref, o_ref):
    x = x_ref[...]
    u = _rms(x, npre_ref[...]).astype(BF16)
    g = jnp.dot(u, wg_ref[...], preferred_element_type=F32)
    up = jnp.dot(u, wu_ref[...], preferred_element_type=F32)
    h = (_silu(g) * up).astype(BF16)
    y = jnp.dot(h, wd_ref[...], preferred_element_type=F32)
    o_ref[...] = x + 0.5 * _rms(y, npost_ref[...])


def _resident(shape):
    return pl.BlockSpec(shape, lambda *_: (0,) * len(shape), pipeline_mode=pl.Buffered(1))


def _ffn(x2, npre, wg, wu, wd, npost, tm):
    t, d = x2.shape
    f = wg.shape[1]
    row = pl.BlockSpec((tm, d), lambda i: (i, 0))
    return pl.pallas_call(
        _ffn_kernel,
        grid=(t // tm,),
        in_specs=[row, _resident((1, d)), _resident((d, f)), _resident((d, f)),
                  _resident((f, d)), _resident((1, d))],
        out_specs=row,
        out_shape=jax.ShapeDtypeStruct((t, d), F32),
        compiler_params=_cparams("parallel"),
        name="ffn",
    )(x2, npre, wg, wu, wd, npost)


def _inproj_kernel(x_ref, npre_ref, w_ref, wab_hi_ref, wab_lo_ref, proj_ref, ab_ref):
    u = _rms(x_ref[...], npre_ref[...])
    u_hi, u_lo = _split2(u)
    proj_ref[...] = jnp.dot(u_hi, w_ref[...], preferred_element_type=F32)
    d = functools.partial(jnp.dot, preferred_element_type=F32)
    ab_ref[...] = d(u_hi, wab_hi_ref[...]) + (d(u_lo, wab_hi_ref[...]) + d(u_hi, wab_lo_ref[...]))


def _inproj(x2, npre, w_main, wab_hi, wab_lo, tm):
    t, d = x2.shape
    n = w_main.shape[1]
    return pl.pallas_call(
        _inproj_kernel,
        grid=(t // tm,),
        in_specs=[pl.BlockSpec((tm, d), lambda i: (i, 0)), _resident((1, d)), _resident((d, n)),
                  _resident((d, LANES)), _resident((d, LANES))],
        out_specs=[pl.BlockSpec((tm, n), lambda i: (i, 0)),
                   pl.BlockSpec((tm, LANES), lambda i: (i, 0))],
        out_shape=[jax.ShapeDtypeStruct((t, n), F32), jax.ShapeDtypeStruct((t, LANES), F32)],
        compiler_params=_cparams("parallel"),
        name="inproj",
    )(x2, npre, w_main, wab_hi, wab_lo)


def _gate_kernel(ab_ref, aexp_ref, dtb_ref, o_ref):
    s = ab_ref.shape[1]
    sup = GDN_SUPER
    ri = lax.broadcasted_iota(jnp.int32, (sup, sup), 0)
    ci = lax.broadcasted_iota(jnp.int32, (sup, sup), 1)
    same = (ri // GDN_CHUNK) == (ci // GDN_CHUNK)
    tril = jnp.where(same & (ci <= ri), 1.0, 0.0).astype(BF16)
    triu = jnp.where(same & (ci >= ri), 1.0, 0.0).astype(BF16)
    col = lax.broadcasted_iota(jnp.int32, (sup, LANES), 1)
    aexp = aexp_ref[...]
    dtb = dtb_ref[...]
    d = functools.partial(jnp.dot, preferred_element_type=F32)

    def body(r, carry):
        t0 = pl.multiple_of(r * sup, sup)
        ab = ab_ref[0, pl.ds(t0, sup), :]
        xs = ab + dtb
        softplus = jnp.maximum(xs, 0.0) + jnp.log1p(jnp.exp(-jnp.abs(xs)))
        g = -aexp * softplus
        beta = jax.nn.sigmoid(ab)
        g_hi, g_mid, g_lo = _split3(g)
        pre = d(tril, g_hi) + (d(tril, g_mid) + d(tril, g_lo))
        suf = d(triu, g_hi) + (d(triu, g_mid) + d(triu, g_lo))
        o_ref[0, pl.ds(t0, sup), :] = jnp.where(col < GDN_HEADS, pre,
                                                jnp.where(col < 2 * GDN_HEADS, suf, beta))
        return carry

    lax.fori_loop(0, s // sup, body, 0)


def _gates(ab, aexp_row, dtb_row):
    b, s, _ = ab.shape
    blk = pl.BlockSpec((1, s, LANES), lambda i: (i, 0, 0))
    vec = pl.BlockSpec((1, LANES), lambda i: (0, 0))
    return pl.pallas_call(
        _gate_kernel,
        grid=(b,),
        in_specs=[blk, vec, vec],
        out_specs=blk,
        out_shape=jax.ShapeDtypeStruct((b, s, LANES), F32),
        compiler_params=_cparams("parallel"),
        name="gdn_gates",
    )(ab, aexp_row, dtb_row)


def _retention_kernel(lgam_ref, rq_ref, rk_ref, rv_ref, rg_ref, cos_ref, sin_ref, o_ref,
                      q_s, k_s, v_s, ob_s, st_s):
    s = rq_ref.shape[1]
    c = RET_CHUNK
    n = s // c
    h = pl.program_id(1)
    lg_f = lgam_ref[0, h]
    lg_b = lgam_ref[1, h]

    def prep(r, carry):
        t0 = pl.multiple_of(r * c, c)
        sl = pl.ds(t0, c)
        cs = cos_ref[0, sl, :]
        sn = sin_ref[0, sl, :]
        q = rq_ref[0, sl, :]
        k = rk_ref[0, sl, :]
        q_s[sl, :] = q * cs + pltpu.roll(q, HEAD_DIM // 2, 1) * sn
        k_s[sl, :] = (k * cs + pltpu.roll(k, HEAD_DIM // 2, 1) * sn) * (HEAD_DIM ** -0.5)
        v_s[sl, :] = rv_ref[0, sl, :]
        return carry

    lax.fori_loop(0, n, prep, 0)

    ri = lax.broadcasted_iota(jnp.int32, (c, c), 0)
    ci = lax.broadcasted_iota(jnp.int32, (c, c), 1)
    rel = (ri - ci).astype(F32)
    m_f = rel >= 0
    m_b = rel < 0
    dmat_f = jnp.where(m_f, jnp.exp(jnp.where(m_f, rel, 0.0) * lg_f), 0.0)
    dmat_b = jnp.where(m_b, jnp.exp(jnp.where(m_b, -rel, 0.0) * lg_b), 0.0)
    pos = lax.broadcasted_iota(jnp.int32, (c, HEAD_DIM), 0).astype(F32)
    qdec_f = jnp.exp((pos + 1.0) * lg_f)
    kdec_f = jnp.exp((c - 1.0 - pos) * lg_f)
    qdec_b = jnp.exp((c - pos) * lg_b)
    kdec_b = jnp.exp(pos * lg_b)
    cd_f = jnp.exp(c * lg_f)
    cd_b = jnp.exp(c * lg_b)

    st_s[...] = jnp.zeros_like(st_s)

    def stream(t0, d, dmat, qdec, kdec, cd, out):
        sl = pl.ds(t0, c)
        q = q_s[sl, :]
        k = k_s[sl, :]
        v = v_s[sl, :]
        st = st_s[d]
        scores = _dot_nt(q, k) * dmat
        out[sl, :] = _dot(scores, v) + _dot(q * qdec, st)
        st_s[d] = st * cd + _dot_tn(k * kdec, v)

    def step(i, carry):
        stream(pl.multiple_of(i * c, c), 0, dmat_f, qdec_f, kdec_f, cd_f, o_ref.at[0])
        stream(pl.multiple_of((n - 1 - i) * c, c), 1, dmat_b, qdec_b, kdec_b, cd_b, ob_s)
        return carry

    lax.fori_loop(0, n, step, 0)

    def fin(r, carry):
        sl = pl.ds(pl.multiple_of(r * c, c), c)
        o = o_ref[0, sl, :] + ob_s[sl, :]
        mu = jnp.mean(o, axis=-1, keepdims=True)
        oc = o - mu
        var = jnp.mean(oc * oc, axis=-1, keepdims=True)
        o_ref[0, sl, :] = _silu(rg_ref[0, sl, :]) * (oc * lax.rsqrt(var + NORM_EPS))
        return carry

    lax.fori_loop(0, n, fin, 0)


def _retention(proj, cosf, sinf, lgam):
    b, s, _ = proj.shape
    hb = lambda off: pl.BlockSpec((1, s, HEAD_DIM), lambda i, j, off=off: (i, 0, off + j))
    tab = pl.BlockSpec((1, s, HEAD_DIM), lambda i, j: (i, 0, 0))
    return pl.pallas_call(
        _retention_kernel,
        grid=(b, RET_HEADS),
        in_specs=[pl.BlockSpec(memory_space=pltpu.SMEM),
                  hb(0), hb(RET_HEADS), hb(2 * RET_HEADS), hb(3 * RET_HEADS), tab, tab],
        out_specs=pl.BlockSpec((1, s, HEAD_DIM), lambda i, j: (i, 0, j)),
        out_shape=jax.ShapeDtypeStruct((b, s, RET_HEADS * HEAD_DIM), F32),
        scratch_shapes=[pltpu.VMEM((s, HEAD_DIM), F32), pltpu.VMEM((s, HEAD_DIM), F32),
                        pltpu.VMEM((s, HEAD_DIM), F32), pltpu.VMEM((s, HEAD_DIM), F32),
                        pltpu.VMEM((2, HEAD_DIM, HEAD_DIM), F32)],
        compiler_params=_cparams("parallel", "parallel"),
        name="retention",
    )(lgam, proj, proj, proj, proj, cosf, sinf)


def _unit_tri_inverse(m):
    c = m.shape[0]
    eye = jnp.where(lax.broadcasted_iota(jnp.int32, (c, c), 0)
                    == lax.broadcasted_iota(jnp.int32, (c, c), 1), 1.0, 0.0)
    p = eye - m
    x = m
    steps = (c - 1).bit_length() - 1
    for _ in range(steps):
        x = _dot_split(x, x)
        p = p + _dot_split(p, x)
    return p


def _gdn_kernel(q_ref, k_ref, v_ref, z_ref, wq_ref, wk_ref, wv_ref, gcol_ref, grow_ref, nw_ref,
                o_ref, xp_s, q_s, k_s, v_s, ob_s, st_s):
    s = q_ref.shape[1]
    c = GDN_CHUNK
    n = s // c
    halo = CONV_HALO
    rb = 256 if s % 256 == 0 else c
    left = (CONV_K - 1) // 2

    zeros_halo = jnp.zeros((halo, HEAD_DIM), F32)
    xp_s[0:halo, :] = zeros_halo
    xp_s[halo + s:halo + s + halo, :] = zeros_halo

    def conv_into(raw_ref, w_ref, dst, normalise, scale):
        xp_s[halo:halo + s, :] = raw_ref[0]
        w = w_ref[...]

        def body(r, carry):
            t0 = pl.multiple_of(r * rb, rb)
            win = xp_s[pl.ds(t0, rb + 2 * halo), :]
            acc = jnp.zeros((rb, HEAD_DIM), F32)
            for j in range(CONV_K):
                off = halo - left + j
                acc = acc + win[off:off + rb, :] * w[j:j + 1, :]
            y = _silu(acc)
            if normalise:
                y = y * lax.rsqrt(jnp.sum(y * y, axis=-1, keepdims=True) + NORM_EPS)
            if scale != 1.0:
                y = y * scale
            dst[pl.ds(t0, rb), :] = y
            return carry

        lax.fori_loop(0, s // rb, body, 0)

    conv_into(q_ref, wq_ref, q_s, True, HEAD_DIM ** -0.5)
    conv_into(k_ref, wk_ref, k_s, True, 1.0)
    conv_into(v_ref, wv_ref, v_s, False, 1.0)

    ri = lax.broadcasted_iota(jnp.int32, (c, c), 0)
    ci = lax.broadcasted_iota(jnp.int32, (c, c), 1)
    st_s[...] = jnp.zeros_like(st_s)

    def stream(chunk, d, out):
        t0 = pl.multiple_of(chunk * c, c)
        sl = pl.ds(t0, c)
        q = q_s[sl, :]
        k = k_s[sl, :]
        v = v_s[sl, :]
        gates = gcol_ref[sl, :]
        gc = gates[:, d:d + 1]
        beta = gates[:, 2 + d:3 + d]
        gr = grow_ref[chunk][d:d + 1, :]
        if d == 0:
            incl, strict = ci <= ri, ci < ri
            gtot = gc[c - 1:c, :]
        else:
            incl, strict = ci >= ri, ci > ri
            gtot = gc[0:1, :]
        diff = gc - gr
        decay = jnp.where(incl, jnp.exp(jnp.where(incl, diff, 0.0)), 0.0)
        kb = k * beta
        lower = jnp.where(strict, _dot_nt(kb, k) * decay, 0.0)
        tinv = _unit_tri_inverse(lower)
        eg = jnp.exp(gc)
        rhs = jnp.concatenate([v * beta, kb * eg], axis=-1)
        sol = _dot(tinv, rhs)
        u, w = sol[:, :HEAD_DIM], sol[:, HEAD_DIM:]
        qk = _dot_nt(q, k) * decay
        st = st_s[d]
        v_new = u - _dot(w, st)
        out[sl, :] = _dot(q * eg, st) + _dot(qk, v_new)
        st_s[d] = st * jnp.exp(gtot) + _dot_tn(k * jnp.exp(gtot - gc), v_new)

    def step(i, carry):
        stream(i, 0, o_ref.at[0])
        stream(n - 1 - i, 1, ob_s)
        return carry

    lax.fori_loop(0, n, step, 0)

    nw = nw_ref[...]

    def fin(r, carry):
        sl = pl.ds(pl.multiple_of(r * rb, rb), rb)
        o = o_ref[0, sl, :] + ob_s[sl, :]
        o = o * lax.rsqrt(jnp.mean(o * o, axis=-1, keepdims=True) + NORM_EPS) * nw
        o_ref[0, sl, :] = o * _silu(z_ref[0, sl, :])
        return carry

    lax.fori_loop(0, s // rb, fin, 0)


def _gdn(proj, conv_w, gcol, grow, norm_w):
    b, s, _ = proj.shape
    n = s // GDN_CHUNK
    base = 4 * RET_HEADS
    hb = lambda off: pl.BlockSpec((1, s, HEAD_DIM), lambda i, j, off=off: (i, 0, off + j))
    cw = lambda off: pl.BlockSpec((CONV_K, HEAD_DIM), lambda i, j, off=off: (0, off + j))
    seq = pltpu.VMEM((s, HEAD_DIM), F32)
    return pl.pallas_call(
        _gdn_kernel,
        grid=(b, GDN_HEADS),
        in_specs=[hb(base), hb(base + GDN_HEADS), hb(base + 2 * GDN_HEADS), hb(base + 3 * GDN_HEADS),
                  cw(0), cw(GDN_HEADS), cw(2 * GDN_HEADS),
                  pl.BlockSpec((None, None, s, 4), lambda i, j: (i, j, 0, 0)),
                  pl.BlockSpec((None, None, n, 2, GDN_CHUNK), lambda i, j: (i, j, 0, 0, 0)),
                  pl.BlockSpec((1, HEAD_DIM), lambda i, j: (0, 0))],
        out_specs=pl.BlockSpec((1, s, HEAD_DIM), lambda i, j: (i, 0, j)),
        out_shape=jax.ShapeDtypeStruct((b, s, GDN_HEADS * HEAD_DIM), F32),
        scratch_shapes=[pltpu.VMEM((s + 2 * CONV_HALO, HEAD_DIM), F32), seq, seq, seq, seq,
                        pltpu.VMEM((2, HEAD_DIM, HEAD_DIM), F32)],
        compiler_params=_cparams("parallel", "parallel"),
        name="gdn",
    )(proj, proj, proj, proj, conv_w, conv_w, conv_w, gcol, grow, norm_w)


def _outproj_kernel(x_ref, oret_ref, ogdn_ref, wa_ref, wb_ref, npost_ref, o_ref):
    h = _dot(oret_ref[...], wa_ref[...]) + _dot(ogdn_ref[...], wb_ref[...])
    o_ref[...] = x_ref[...] + _rms(h, npost_ref[...])


def _outproj(x2, o_ret, o_gdn, w_a, w_b, npost, tm):
    t, d = x2.shape
    wr = o_ret.shape[1]
    wg = o_gdn.shape[1]
    row = pl.BlockSpec((tm, d), lambda i: (i, 0))
    return pl.pallas_call(
        _outproj_kernel,
        grid=(t // tm,),
        in_specs=[row, pl.BlockSpec((tm, wr), lambda i: (i, 0)), pl.BlockSpec((tm, wg), lambda i: (i, 0)),
                  _resident((wr, d)), _resident((wg, d)), _resident((1, d))],
        out_specs=row,
        out_shape=jax.ShapeDtypeStruct((t, d), F32),
        compiler_params=_cparams("parallel"),
        name="outproj",
    )(x2, o_ret, o_gdn, w_a, w_b, npost)


def _memkv_kernel(mem_ref, mw_ref, wk_ref, wv_ref, k_ref, v_ref):
    m = _rms(mem_ref[0], mw_ref[...]).astype(BF16)
    k_ref[0] = jnp.dot(m, wk_ref[...], preferred_element_type=F32).astype(BF16)
    v_ref[0] = jnp.dot(m, wv_ref[...], preferred_element_type=F32).astype(BF16)


def _memkv(mem, mw, wk, wv):
    b, m, d = mem.shape
    blk = pl.BlockSpec((1, m, d), lambda i: (i, 0, 0))
    out = jax.ShapeDtypeStruct((b, m, d), BF16)
    return pl.pallas_call(
        _memkv_kernel,
        grid=(b,),
        in_specs=[blk, _resident((1, d)), _resident((d, d)), _resident((d, d))],
        out_specs=[blk, blk],
        out_shape=[out, out],
        compiler_params=_cparams("parallel"),
        name="mem_kv",
    )(mem, mw, wk, wv)


def _xattn_kernel(x_ref, npre_ref, wq_ref, k_ref, v_ref, wo_ref, npost_ref, o_ref):
    x = x_ref[0]
    d = x.shape[-1]
    hd = d // XATTN_HEADS
    u = _rms(x, npre_ref[...]).astype(BF16)
    q = jnp.dot(u, wq_ref[...], preferred_element_type=F32).astype(BF16)
    heads = []
    for h in range(XATTN_HEADS):
        sl = slice(h * hd, (h + 1) * hd)
        sc = lax.dot_general(q[:, sl], k_ref[0, :, sl], (((1,), (1,)), ((), ())),
                             preferred_element_type=F32) * (hd ** -0.5)
        e = jnp.exp(sc - jnp.max(sc, axis=-1, keepdims=True))
        p = e / jnp.sum(e, axis=-1, keepdims=True)
        heads.append(jnp.dot(p.astype(BF16), v_ref[0, :, sl], preferred_element_type=F32))
    o = jnp.concatenate(heads, axis=-1).astype(BF16)
    hh = jnp.dot(o, wo_ref[...], preferred_element_type=F32)
    o_ref[0] = x + _rms(hh, npost_ref[...])


def _xattn(x, npre, wq, kmem, vmem, wo, npost, tm):
    b, s, d = x.shape
    m = kmem.shape[1]
    row = pl.BlockSpec((1, tm, d), lambda i, j: (i, j, 0))
    kv = pl.BlockSpec((1, m, d), lambda i, j: (i, 0, 0))
    return pl.pallas_call(
        _xattn_kernel,
        grid=(b, s // tm),
        in_specs=[row, _resident((1, d)), _resident((d, d)), kv, kv, _resident((d, d)),
                  _resident((1, d))],
        out_specs=row,
        out_shape=jax.ShapeDtypeStruct((b, s, d), F32),
        compiler_params=_cparams("parallel", "parallel"),
        name="xattn",
    )(x, npre, wq, kmem, vmem, wo, npost)


def _pad_lanes(v):
    return jnp.pad(v.reshape(1, -1).astype(F32), ((0, 0), (0, LANES - v.size)))


def kernel(x, mem, positions, norm_pre, norm_post, mem_norm, ffn1_gate, ffn1_up, ffn1_down, w_in, gdn_conv, ret_log_gamma, gdn_a_log, gdn_dt_bias, gdn_norm, w_out, xattn_q, xattn_k, xattn_v, xattn_o, ffn2_gate, ffn2_up, ffn2_down):
    b, s, d = x.shape
    depth = norm_pre.shape[0]
    t = b * s
    tm = 512 if s % 512 == 0 else s
    n_chunks = s // GDN_CHUNK
    ret_w = RET_HEADS * HEAD_DIM
    main_cols = 4 * ret_w + 4 * GDN_HEADS * HEAD_DIM
    bf = lambda w: w.astype(BF16)
    vec = lambda w: w.reshape(1, -1).astype(F32)

    cosf, sinf = _rope_tables(positions)
    x2 = x.reshape(t, d)
    for l in range(depth):
        x2 = _ffn(x2, vec(norm_pre[l, 0]), bf(ffn1_gate[l]), bf(ffn1_up[l]), bf(ffn1_down[l]),
                  vec(norm_post[l, 0]), tm)

        wab = jnp.pad(w_in[l][:, main_cols:], ((0, 0), (0, LANES - 4 * GDN_HEADS)))
        wab_hi = wab.astype(BF16)
        wab_lo = (wab - wab_hi.astype(F32)).astype(BF16)
        proj, ab = _inproj(x2, vec(norm_pre[l, 1]), bf(w_in[l][:, :main_cols]), wab_hi, wab_lo, tm)
        proj = proj.reshape(b, s, main_cols)
        gates = _gates(ab.reshape(b, s, LANES), _pad_lanes(jnp.exp(gdn_a_log[l].astype(F32))),
                       _pad_lanes(gdn_dt_bias[l]))
        gcol = gates[:, :, :4 * GDN_HEADS].reshape(b, s, 4, GDN_HEADS).transpose(0, 3, 1, 2)
        grow = gates[:, :, :2 * GDN_HEADS].reshape(b, n_chunks, GDN_CHUNK, 2, GDN_HEADS)
        grow = grow.transpose(0, 4, 1, 3, 2)
        o_ret = _retention(proj, cosf, sinf, ret_log_gamma[l].astype(F32))
        o_gdn = _gdn(proj, gdn_conv[l].astype(F32), gcol, grow, vec(gdn_norm[l]))
        x2 = _outproj(x2, o_ret.reshape(t, ret_w), o_gdn.reshape(t, -1),
                      bf(w_out[l][:ret_w]), bf(w_out[l][ret_w:]), vec(norm_post[l, 1]), tm)

        kmem, vmem = _memkv(mem, vec(mem_norm[l]), bf(xattn_k[l]), bf(xattn_v[l]))
        x3 = _xattn(x2.reshape(b, s, d), vec(norm_pre[l, 2]), bf(xattn_q[l]), kmem, vmem,
                    bf(xattn_o[l]), vec(norm_post[l, 2]), tm)
        x2 = x3.reshape(t, d)

        x2 = _ffn(x2, vec(norm_pre[l, 3]), bf(ffn2_gate[l]), bf(ffn2_up[l]), bf(ffn2_down[l]),
                  vec(norm_post[l, 3]), tm)
    return x2.reshape(b, s, d)
```

```python
import functools

import jax
import jax.numpy as jnp
from jax import lax
from jax.experimental import pallas as pl
from jax.experimental.pallas import tpu as pltpu

F32 = jnp.float32
BF16 = jnp.bfloat16

HEAD_DIM = 128
RET_HEADS = 4
RET_CHUNK = 128
GDN_HEADS = 4
GDN_CHUNK = 64
GDN_SUPER = 256
GDN_BASE = 8
CONV_K = 5
CONV_HALO = 8
XATTN_HEADS = 4
ROPE_BASE = 10000.0
NORM_EPS = 1e-6
LANES = 128
VMEM_LIMIT = 56 * 1024 * 1024


def _cparams(*sem):
    return pltpu.CompilerParams(dimension_semantics=sem, vmem_limit_bytes=VMEM_LIMIT)


def _rms(x, w):
    return x * lax.rsqrt(jnp.mean(x * x, axis=-1, keepdims=True) + NORM_EPS) * w


def _silu(x):
    return x * jax.nn.sigmoid(x)


def _dot(a, b):
    return jnp.dot(a.astype(BF16), b.astype(BF16), preferred_element_type=F32)


def _dot_nt(a, b):
    return lax.dot_general(a.astype(BF16), b.astype(BF16), (((1,), (1,)), ((), ())),
                           preferred_element_type=F32)


def _dot_tn(a, b):
    return lax.dot_general(a.astype(BF16), b.astype(BF16), (((0,), (0,)), ((), ())),
                           preferred_element_type=F32)


def _split2(a):
    hi = a.astype(BF16)
    lo = (a - hi.astype(F32)).astype(BF16)
    return hi, lo


def _split3(a):
    hi = a.astype(BF16)
    r = a - hi.astype(F32)
    mid = r.astype(BF16)
    lo = (r - mid.astype(F32)).astype(BF16)
    return hi, mid, lo


def _rope_kernel(pos_ref, invf_ref, sign_ref, cos_ref, sin_ref):
    ang = pos_ref[0].astype(F32) * invf_ref[...]
    cos_ref[0] = jnp.cos(ang)
    sin_ref[0] = jnp.sin(ang) * sign_ref[...]


def _rope_tables(positions):
    b, s = positions.shape
    ts = min(s, 512)
    half = HEAD_DIM // 2
    inv_freq = ROPE_BASE ** (-jnp.arange(0, HEAD_DIM, 2, dtype=F32) / HEAD_DIM)
    invf = jnp.concatenate([inv_freq, inv_freq])[None, :]
    sign = jnp.concatenate([-jnp.ones((half,), F32), jnp.ones((half,), F32)])[None, :]
    tab = jax.ShapeDtypeStruct((b, s, HEAD_DIM), F32)
    return pl.pallas_call(
        _rope_kernel,
        grid=(b, s // ts),
        in_specs=[pl.BlockSpec((1, ts, 1), lambda i, j: (i, j, 0)),
                  pl.BlockSpec((1, HEAD_DIM), lambda i, j: (0, 0)),
                  pl.BlockSpec((1, HEAD_DIM), lambda i, j: (0, 0))],
        out_specs=[pl.BlockSpec((1, ts, HEAD_DIM), lambda i, j: (i, j, 0)),
                   pl.BlockSpec((1, ts, HEAD_DIM), lambda i, j: (i, j, 0))],
        out_shape=[tab, tab],
        compiler_params=_cparams("parallel", "parallel"),
        name="rope_tables",
    )(positions.reshape(b, s, 1), invf, sign)


def _ffn_kernel(x_ref, npre_ref, wg_ref, wu_ref, wd_ref, npost_ref, o_ref):
    x = x_ref[...]
    u = _rms(x, npre_ref[...]).astype(BF16)
    g = jnp.dot(u, wg_ref[...], preferred_element_type=F32)
    up = jnp.dot(u, wu_ref[...], preferred_element_type=F32)
    h = (_silu(g) * up).astype(BF16)
    y = jnp.dot(h, wd_ref[...], preferred_element_type=F32)
    o_ref[...] = x + 0.5 * _rms(y, npost_ref[...])


def _resident(shape):
    return pl.BlockSpec(shape, lambda *_: (0,) * len(shape), pipeline_mode=pl.Buffered(1))


def _ffn(x2, npre, wg, wu, wd, npost, tm):
    t, d = x2.shape
    f = wg.shape[1]
    row = pl.BlockSpec((tm, d), lambda i: (i, 0))
    return pl.pallas_call(
        _ffn_kernel,
        grid=(t // tm,),
        in_specs=[row, _resident((1, d)), _resident((d, f)), _resident((d, f)),
                  _resident((f, d)), _resident((1, d))],
        out_specs=row,
        out_shape=jax.ShapeDtypeStruct((t, d), F32),
        compiler_params=_cparams("parallel"),
        name="ffn",
    )(x2, npre, wg, wu, wd, npost)


def _inproj_kernel(x_ref, npre_ref, w_ref, wab_hi_ref, wab_lo_ref, proj_ref, ab_ref):
    u = _rms(x_ref[...], npre_ref[...])
    u_hi, u_lo = _split2(u)
    proj_ref[...] = jnp.dot(u_hi, w_ref[...], preferred_element_type=F32)
    d = functools.partial(jnp.dot, preferred_element_type=F32)
    ab_ref[...] = d(u_hi, wab_hi_ref[...]) + (d(u_lo, wab_hi_ref[...]) + d(u_hi, wab_lo_ref[...]))


def _inproj(x2, npre, w_main, wab_hi, wab_lo, tm):
    t, d = x2.shape
    n = w_main.shape[1]
    return pl.pallas_call(
        _inproj_kernel,
        grid=(t // tm,),
        in_specs=[pl.BlockSpec((tm, d), lambda i: (i, 0)), _resident((1, d)), _resident((d, n)),
                  _resident((d, LANES)), _resident((d, LANES))],
        out_specs=[pl.BlockSpec((tm, n), lambda i: (i, 0)),
                   pl.BlockSpec((tm, LANES), lambda i: (i, 0))],
        out_shape=[jax.ShapeDtypeStruct((t, n), F32), jax.ShapeDtypeStruct((t, LANES), F32)],
        compiler_params=_cparams("parallel"),
        name="inproj",
    )(x2, npre, w_main, wab_hi, wab_lo)


def _gate_kernel(ab_ref, aexp_ref, dtb_ref, o_ref):
    s = ab_ref.shape[1]
    sup = GDN_SUPER
    ri = lax.broadcasted_iota(jnp.int32, (sup, sup), 0)
    ci = lax.broadcasted_iota(jnp.int32, (sup, sup), 1)
    same = (ri // GDN_CHUNK) == (ci // GDN_CHUNK)
    tril = jnp.where(same & (ci <= ri), 1.0, 0.0).astype(BF16)
    triu = jnp.where(same & (ci >= ri), 1.0, 0.0).astype(BF16)
    col = lax.broadcasted_iota(jnp.int32, (sup, LANES), 1)
    aexp = aexp_ref[...]
    dtb = dtb_ref[...]
    d = functools.partial(jnp.dot, preferred_element_type=F32)

    def body(r, carry):
        t0 = pl.multiple_of(r * sup, sup)
        ab = ab_ref[0, pl.ds(t0, sup), :]
        xs = ab + dtb
        softplus = jnp.maximum(xs, 0.0) + jnp.log1p(jnp.exp(-jnp.abs(xs)))
        g = -aexp * softplus
        beta = jax.nn.sigmoid(ab)
        g_hi, g_mid, g_lo = _split3(g)
        pre = d(tril, g_hi) + (d(tril, g_mid) + d(tril, g_lo))
        suf = d(triu, g_hi) + (d(triu, g_mid) + d(triu, g_lo))
        o_ref[0, pl.ds(t0, sup), :] = jnp.where(col < GDN_HEADS, pre,
                                                jnp.where(col < 2 * GDN_HEADS, suf, beta))
        return carry

    lax.fori_loop(0, s // sup, body, 0)


def _gates(ab, aexp_row, dtb_row):
    b, s, _ = ab.shape
    blk = pl.BlockSpec((1, s, LANES), lambda i: (i, 0, 0))
    vec = pl.BlockSpec((1, LANES), lambda i: (0, 0))
    return pl.pallas_call(
        _gate_kernel,
        grid=(b,),
        in_specs=[blk, vec, vec],
        out_specs=blk,
        out_shape=jax.ShapeDtypeStruct((b, s, LANES), F32),
        compiler_params=_cparams("parallel"),
        name="gdn_gates",
    )(ab, aexp_row, dtb_row)


def _retention_kernel(lgam_ref, rq_ref, rk_ref, rv_ref, rg_ref, cos_ref, sin_ref, o_ref,
                      q_s, k_s, v_s, ob_s, st_s):
    s = rq_ref.shape[1]
    c = RET_CHUNK
    n = s // c
    h = pl.program_id(1)
    lg_f = lgam_ref[0, h]
    lg_b = lgam_ref[1, h]

    def prep(r, carry):
        t0 = pl.multiple_of(r * c, c)
        sl = pl.ds(t0, c)
        cs = cos_ref[0, sl, :]
        sn = sin_ref[0, sl, :]
        q = rq_ref[0, sl, :]
        k = rk_ref[0, sl, :]
        q_s[sl, :] = q * cs + pltpu.roll(q, HEAD_DIM // 2, 1) * sn
        k_s[sl, :] = (k * cs + pltpu.roll(k, HEAD_DIM // 2, 1) * sn) * (HEAD_DIM ** -0.5)
        v_s[sl, :] = rv_ref[0, sl, :]
        return carry

    lax.fori_loop(0, n, prep, 0)

    ri = lax.broadcasted_iota(jnp.int32, (c, c), 0)
    ci = lax.broadcasted_iota(jnp.int32, (c, c), 1)
    rel = (ri - ci).astype(F32)
    m_f = rel >= 0
    m_b = rel < 0
    dmat_f = jnp.where(m_f, jnp.exp(jnp.where(m_f, rel, 0.0) * lg_f), 0.0)
    dmat_b = jnp.where(m_b, jnp.exp(jnp.where(m_b, -rel, 0.0) * lg_b), 0.0)
    pos = lax.broadcasted_iota(jnp.int32, (c, HEAD_DIM), 0).astype(F32)
    qdec_f = jnp.exp((pos + 1.0) * lg_f)
    kdec_f = jnp.exp((c - 1.0 - pos) * lg_f)
    qdec_b = jnp.exp((c - pos) * lg_b)
    kdec_b = jnp.exp(pos * lg_b)
    cd_f = jnp.exp(c * lg_f)
    cd_b = jnp.exp(c * lg_b)

    st_s[...] = jnp.zeros_like(st_s)

    def stream(t0, d, dmat, qdec, kdec, cd, out):
        sl = pl.ds(t0, c)
        q = q_s[sl, :]
        k = k_s[sl, :]
        v = v_s[sl, :]
        st = st_s[d]
        scores = _dot_nt(q, k) * dmat
        out[sl, :] = _dot(scores, v) + _dot(q * qdec, st)
        st_s[d] = st * cd + _dot_tn(k * kdec, v)

    def step(i, carry):
        stream(pl.multiple_of(i * c, c), 0, dmat_f, qdec_f, kdec_f, cd_f, o_ref.at[0])
        stream(pl.multiple_of((n - 1 - i) * c, c), 1, dmat_b, qdec_b, kdec_b, cd_b, ob_s)
        return carry

    lax.fori_loop(0, n, step, 0, unroll=2)

    def fin(r, carry):
        sl = pl.ds(pl.multiple_of(r * c, c), c)
        o = o_ref[0, sl, :] + ob_s[sl, :]
        mu = jnp.mean(o, axis=-1, keepdims=True)
        oc = o - mu
        var = jnp.mean(oc * oc, axis=-1, keepdims=True)
        o_ref[0, sl, :] = _silu(rg_ref[0, sl, :]) * (oc * lax.rsqrt(var + NORM_EPS))
        return carry

    lax.fori_loop(0, n, fin, 0)


def _retention(proj, cosf, sinf, lgam):
    b, s, _ = proj.shape
    hb = lambda off: pl.BlockSpec((1, s, HEAD_DIM), lambda i, j, off=off: (i, 0, off + j))
    tab = pl.BlockSpec((1, s, HEAD_DIM), lambda i, j: (i, 0, 0))
    return pl.pallas_call(
        _retention_kernel,
        grid=(b, RET_HEADS),
        in_specs=[pl.BlockSpec(memory_space=pltpu.SMEM),
                  hb(0), hb(RET_HEADS), hb(2 * RET_HEADS), hb(3 * RET_HEADS), tab, tab],
        out_specs=pl.BlockSpec((1, s, HEAD_DIM), lambda i, j: (i, 0, j)),
        out_shape=jax.ShapeDtypeStruct((b, s, RET_HEADS * HEAD_DIM), F32),
        scratch_shapes=[pltpu.VMEM((s, HEAD_DIM), F32), pltpu.VMEM((s, HEAD_DIM), F32),
                        pltpu.VMEM((s, HEAD_DIM), F32), pltpu.VMEM((s, HEAD_DIM), F32),
                        pltpu.VMEM((2, HEAD_DIM, HEAD_DIM), F32)],
        compiler_params=_cparams("parallel", "parallel"),
        name="retention",
    )(lgam, proj, proj, proj, proj, cosf, sinf)


def _block_diag(xs, same):
    xb = xs.astype(BF16)
    reps = same.shape[0] // xs.shape[0]
    return jnp.where(same, jnp.concatenate([xb] * reps, axis=0), jnp.zeros((), BF16))


def _stack_blocks(bd, c):
    out = bd[0:c]
    for j in range(1, bd.shape[0] // c):
        out = out + bd[j * c:(j + 1) * c]
    return out


def _unit_tri_inverse_bd(lbd, same, eye_s, blk):
    c = eye_s.shape[0]
    dot = functools.partial(jnp.dot, preferred_element_type=F32)
    lb = lbd.astype(BF16)
    zero = jnp.zeros((), BF16)
    xs = _stack_blocks(jnp.where(blk(GDN_BASE), lbd, 0.0), c)
    ps = eye_s - xs
    xbd = _block_diag(xs, same)
    for _ in range((GDN_BASE - 1).bit_length() - 1):
        xs = dot(xs.astype(BF16), xbd)
        xbd = _block_diag(xs, same)
        ps = ps + dot(ps.astype(BF16), xbd)
    b = GDN_BASE
    while b < c:
        off = jnp.where(blk(2 * b) & ~blk(b), lb, zero)
        y = dot(ps.astype(BF16), off)
        ps = ps - dot(y.astype(BF16), _block_diag(ps, same))
        b *= 2
    return _block_diag(ps, same)


def _gdn_kernel(q_ref, k_ref, v_ref, z_ref, wq_ref, wk_ref, wv_ref, gcol_ref, grow_ref, nw_ref,
                o_ref, u_s, w_s, qg_s, kt_s, m2_s, ob_s, st_s):
    s = q_ref.shape[1]
    c = GDN_CHUNK
    n = s // c
    sup = GDN_SUPER
    nsup = s // sup
    per = sup // c
    halo = CONV_HALO
    left = (CONV_K - 1) // 2

    ri = lax.broadcasted_iota(jnp.int32, (sup, sup), 0)
    ci = lax.broadcasted_iota(jnp.int32, (sup, sup), 1)
    blk = lambda size: (ri // size) == (ci // size)
    same = blk(c)
    masks = ((same & (ci <= ri), same & (ci < ri)), (same & (ci >= ri), same & (ci > ri)))
    eye_s = jnp.where(lax.broadcasted_iota(jnp.int32, (c, sup), 0)
                      == lax.broadcasted_iota(jnp.int32, (c, sup), 1) % c, 1.0, 0.0)

    def conv_tile(raw_ref, w_ref, r, t0):
        w = w_ref[...]
        p0 = pl.multiple_of(jnp.maximum(t0 - halo, 0), halo)
        n0 = pl.multiple_of(jnp.minimum(t0 + sup, s - halo), halo)
        prev = jnp.where(r > 0, raw_ref[0, pl.ds(p0, halo), :], 0.0)
        nxt = jnp.where(r < nsup - 1, raw_ref[0, pl.ds(n0, halo), :], 0.0)
        win = jnp.concatenate([prev, raw_ref[0, pl.ds(t0, sup), :], nxt], axis=0)
        acc = jnp.zeros((sup, HEAD_DIM), F32)
        for j in range(CONV_K):
            off = halo - left + j
            acc = acc + win[off:off + sup, :] * w[j:j + 1, :]
        return _silu(acc)

    def l2n(y):
        return y * lax.rsqrt(jnp.sum(y * y, axis=-1, keepdims=True) + NORM_EPS)

    def phase_a(r, carry):
        t0 = pl.multiple_of(r * sup, sup)
        rows = pl.ds(t0, sup)
        q = l2n(conv_tile(q_ref, wq_ref, r, t0)) * (HEAD_DIM ** -0.5)
        k = l2n(conv_tile(k_ref, wk_ref, r, t0))
        v = conv_tile(v_ref, wv_ref, r, t0)
        gram = _dot_nt(jnp.concatenate([q, k], axis=0), k)
        qk, kk = gram[:sup], gram[sup:]
        gates = gcol_ref[rows, :]
        grw = grow_ref[r]
        for d in range(2):
            incl, strict = masks[d]
            gc = gates[:, d:d + 1]
            beta = gates[:, 2 + d:3 + d]
            decay = jnp.where(incl, jnp.exp(jnp.where(incl, gc - grw[d:d + 1, :], 0.0)), 0.0)
            lbd = jnp.where(strict, kk * decay, 0.0) * beta
            tbd = _unit_tri_inverse_bd(lbd, same, eye_s, blk)
            eg = jnp.exp(gc)
            rhs = jnp.concatenate([v * beta, k * (beta * eg)], axis=-1).astype(BF16)
            sol = jnp.dot(tbd, rhs, preferred_element_type=F32)
            u_s[d, rows, :] = sol[:, :HEAD_DIM].astype(BF16)
            w_s[d, rows, :] = sol[:, HEAD_DIM:].astype(BF16)
            qkd = qk * decay
            m2_s[d, rows, :] = jnp.concatenate(
                [qkd[p * LANES:(p + 1) * LANES, p * LANES:(p + 1) * LANES] for p in range(sup // LANES)],
                axis=0).astype(BF16)
            qg_s[d, rows, :] = (q * eg).astype(BF16)
            edge = (lambda j: j * c + c - 1) if d == 0 else (lambda j: j * c)
            gtot = jnp.concatenate([jnp.broadcast_to(gc[edge(j):edge(j) + 1, :], (c, 1))
                                    for j in range(per)], axis=0)
            kt_s[d, rows, :] = (k * jnp.exp(gtot - gc)).astype(BF16)
        return carry

    lax.fori_loop(0, nsup, phase_a, 0, unroll=2)

    st_s[...] = jnp.zeros_like(st_s)

    def chunk_step(chunk, d, out):
        t0 = pl.multiple_of(chunk * c, c)
        rows = pl.ds(t0, c)
        st = st_s[d]
        both = jnp.dot(jnp.concatenate([w_s[d, rows, :], qg_s[d, rows, :]], axis=0), st.astype(BF16),
                       preferred_element_type=F32)
        vb = (u_s[d, rows, :].astype(F32) - both[:c]).astype(BF16)
        vv = jnp.concatenate([vb, vb], axis=0)
        out[rows, :] = both[c:] + jnp.dot(m2_s[d, rows, :], vv, preferred_element_type=F32)
        gtot = gcol_ref[pl.ds(t0 + (c - 1 if d == 0 else 0), 1), d:d + 1]
        st_s[d] = st * jnp.exp(gtot) + _dot_tn(kt_s[d, rows, :], vb)

    def phase_b(i, carry):
        chunk_step(i, 0, o_ref.at[0])
        chunk_step(n - 1 - i, 1, ob_s)
        return carry

    lax.fori_loop(0, n, phase_b, 0, unroll=2)

    nw = nw_ref[...]

    def fin(r, carry):
        sl = pl.ds(pl.multiple_of(r * sup, sup), sup)
        o = o_ref[0, sl, :] + ob_s[sl, :]
        o = o * lax.rsqrt(jnp.mean(o * o, axis=-1, keepdims=True) + NORM_EPS) * nw
        o_ref[0, sl, :] = o * _silu(z_ref[0, sl, :])
        return carry

    lax.fori_loop(0, nsup, fin, 0)


def _gdn(proj, conv_w, gcol, grow, norm_w):
    b, s, _ = proj.shape
    assert s % GDN_SUPER == 0
    base = 4 * RET_HEADS
    hb = lambda off: pl.BlockSpec((1, s, HEAD_DIM), lambda i, j, off=off: (i, 0, off + j))
    cw = lambda off: pl.BlockSpec((CONV_K, HEAD_DIM), lambda i, j, off=off: (0, off + j))
    seq2 = pltpu.VMEM((2, s, HEAD_DIM), BF16)
    return pl.pallas_call(
        _gdn_kernel,
        grid=(b, GDN_HEADS),
        in_specs=[hb(base), hb(base + GDN_HEADS), hb(base + 2 * GDN_HEADS), hb(base + 3 * GDN_HEADS),
                  cw(0), cw(GDN_HEADS), cw(2 * GDN_HEADS),
                  pl.BlockSpec((None, None, s, 4), lambda i, j: (i, j, 0, 0)),
                  pl.BlockSpec((None, None, s // GDN_SUPER, 2, GDN_SUPER), lambda i, j: (i, j, 0, 0, 0)),
                  pl.BlockSpec((1, HEAD_DIM), lambda i, j: (0, 0))],
        out_specs=pl.BlockSpec((1, s, HEAD_DIM), lambda i, j: (i, 0, j)),
        out_shape=jax.ShapeDtypeStruct((b, s, GDN_HEADS * HEAD_DIM), F32),
        scratch_shapes=[seq2, seq2, seq2, seq2, seq2, pltpu.VMEM((s, HEAD_DIM), F32),
                        pltpu.VMEM((2, HEAD_DIM, HEAD_DIM), F32)],
        compiler_params=_cparams("parallel", "parallel"),
        name="gdn",
    )(proj, proj, proj, proj, conv_w, conv_w, conv_w, gcol, grow, norm_w)


def _outproj_kernel(x_ref, oret_ref, ogdn_ref, wa_ref, wb_ref, npost_ref, o_ref):
    h = _dot(oret_ref[...], wa_ref[...]) + _dot(ogdn_ref[...], wb_ref[...])
    o_ref[...] = x_ref[...] + _rms(h, npost_ref[...])


def _outproj(x2, o_ret, o_gdn, w_a, w_b, npost, tm):
    t, d = x2.shape
    wr = o_ret.shape[1]
    wg = o_gdn.shape[1]
    row = pl.BlockSpec((tm, d), lambda i: (i, 0))
    return pl.pallas_call(
        _outproj_kernel,
        grid=(t // tm,),
        in_specs=[row, pl.BlockSpec((tm, wr), lambda i: (i, 0)), pl.BlockSpec((tm, wg), lambda i: (i, 0)),
                  _resident((wr, d)), _resident((wg, d)), _resident((1, d))],
        out_specs=row,
        out_shape=jax.ShapeDtypeStruct((t, d), F32),
        compiler_params=_cparams("parallel"),
        name="outproj",
    )(x2, o_ret, o_gdn, w_a, w_b, npost)


def _memkv_kernel(mem_ref, mw_ref, wk_ref, wv_ref, k_ref, v_ref):
    m = _rms(mem_ref[0], mw_ref[...]).astype(BF16)
    k_ref[0] = jnp.dot(m, wk_ref[...], preferred_element_type=F32).astype(BF16)
    v_ref[0] = jnp.dot(m, wv_ref[...], preferred_element_type=F32).astype(BF16)


def _memkv(mem, mw, wk, wv):
    b, m, d = mem.shape
    blk = pl.BlockSpec((1, m, d), lambda i: (i, 0, 0))
    out = jax.ShapeDtypeStruct((b, m, d), BF16)
    return pl.pallas_call(
        _memkv_kernel,
        grid=(b,),
        in_specs=[blk, _resident((1, d)), _resident((d, d)), _resident((d, d))],
        out_specs=[blk, blk],
        out_shape=[out, out],
        compiler_params=_cparams("parallel"),
        name="mem_kv",
    )(mem, mw, wk, wv)


def _xattn_kernel(x_ref, npre_ref, wq_ref, k_ref, v_ref, wo_ref, npost_ref, o_ref):
    x = x_ref[0]
    d = x.shape[-1]
    hd = d // XATTN_HEADS
    u = _rms(x, npre_ref[...]).astype(BF16)
    q = jnp.dot(u, wq_ref[...], preferred_element_type=F32).astype(BF16)
    heads = []
    for h in range(XATTN_HEADS):
        sl = slice(h * hd, (h + 1) * hd)
        sc = lax.dot_general(q[:, sl], k_ref[0, :, sl], (((1,), (1,)), ((), ())),
                             preferred_element_type=F32) * (hd ** -0.5)
        e = jnp.exp(sc - jnp.max(sc, axis=-1, keepdims=True))
        p = e / jnp.sum(e, axis=-1, keepdims=True)
        heads.append(jnp.dot(p.astype(BF16), v_ref[0, :, sl], preferred_element_type=F32))
    o = jnp.concatenate(heads, axis=-1).astype(BF16)
    hh = jnp.dot(o, wo_ref[...], preferred_element_type=F32)
    o_ref[0] = x + _rms(hh, npost_ref[...])


def _xattn(x, npre, wq, kmem, vmem, wo, npost, tm):
    b, s, d = x.shape
    m = kmem.shape[1]
    row = pl.BlockSpec((1, tm, d), lambda i, j: (i, j, 0))
    kv = pl.BlockSpec((1, m, d), lambda i, j: (i, 0, 0))
    return pl.pallas_call(
        _xattn_kernel,
        grid=(b, s // tm),
        in_specs=[row, _resident((1, d)), _resident((d, d)), kv, kv, _resident((d, d)),
                  _resident((1, d))],
        out_specs=row,
        out_shape=jax.ShapeDtypeStruct((b, s, d), F32),
        compiler_params=_cparams("parallel", "parallel"),
        name="xattn",
    )(x, npre, wq, kmem, vmem, wo, npost)


def _pad_lanes(v):
    return jnp.pad(v.reshape(1, -1).astype(F32), ((0, 0), (0, LANES - v.size)))


def kernel(x, mem, positions, norm_pre, norm_post, mem_norm, ffn1_gate, ffn1_up, ffn1_down, w_in, gdn_conv, ret_log_gamma, gdn_a_log, gdn_dt_bias, gdn_norm, w_out, xattn_q, xattn_k, xattn_v, xattn_o, ffn2_gate, ffn2_up, ffn2_down):
    b, s, d = x.shape
    depth = norm_pre.shape[0]
    t = b * s
    tm = 512 if s % 512 == 0 else s
    ret_w = RET_HEADS * HEAD_DIM
    main_cols = 4 * ret_w + 4 * GDN_HEADS * HEAD_DIM
    bf = lambda w: w.astype(BF16)
    vec = lambda w: w.reshape(1, -1).astype(F32)

    cosf, sinf = _rope_tables(positions)
    x2 = x.reshape(t, d)
    for l in range(depth):
        x2 = _ffn(x2, vec(norm_pre[l, 0]), bf(ffn1_gate[l]), bf(ffn1_up[l]), bf(ffn1_down[l]),
                  vec(norm_post[l, 0]), tm)

        wab = jnp.pad(w_in[l][:, main_cols:], ((0, 0), (0, LANES - 4 * GDN_HEADS)))
        wab_hi = wab.astype(BF16)
        wab_lo = (wab - wab_hi.astype(F32)).astype(BF16)
        proj, ab = _inproj(x2, vec(norm_pre[l, 1]), bf(w_in[l][:, :main_cols]), wab_hi, wab_lo, tm)
        proj = proj.reshape(b, s, main_cols)
        gates = _gates(ab.reshape(b, s, LANES), _pad_lanes(jnp.exp(gdn_a_log[l].astype(F32))),
                       _pad_lanes(gdn_dt_bias[l]))
        gcol = gates[:, :, :4 * GDN_HEADS].reshape(b, s, 4, GDN_HEADS).transpose(0, 3, 1, 2)
        grow = gates[:, :, :2 * GDN_HEADS].reshape(b, s // GDN_SUPER, GDN_SUPER, 2, GDN_HEADS)
        grow = grow.transpose(0, 4, 1, 3, 2)
        o_ret = _retention(proj, cosf, sinf, ret_log_gamma[l].astype(F32))
        o_gdn = _gdn(proj, gdn_conv[l].astype(F32), gcol, grow, vec(gdn_norm[l]))
        x2 = _outproj(x2, o_ret.reshape(t, ret_w), o_gdn.reshape(t, -1),
                      bf(w_out[l][:ret_w]), bf(w_out[l][ret_w:]), vec(norm_post[l, 1]), tm)

        kmem, vmem = _memkv(mem, vec(mem_norm[l]), bf(xattn_k[l]), bf(xattn_v[l]))
        x3 = _xattn(x2.reshape(b, s, d), vec(norm_pre[l, 2]), bf(xattn_q[l]), kmem, vmem,
                    bf(xattn_o[l]), vec(norm_post[l, 2]), tm)
        x2 = x3.reshape(t, d)

        x2 = _ffn(x2, vec(norm_pre[l, 3]), bf(ffn2_gate[l]), bf(ffn2_up[l]), bf(ffn2_down[l]),
                  vec(norm_post[l, 3]), tm)
    return x2.reshape(b, s, d)
```

```python
import functools

import jax
import jax.numpy as jnp
from jax import lax
from jax.experimental import pallas as pl
from jax.experimental.pallas import tpu as pltpu

F32 = jnp.float32
BF16 = jnp.bfloat16

HEAD_DIM = 128
RET_HEADS = 4
RET_CHUNK = 128
RET_GROUP = 4
GDN_HEADS = 4
GDN_CHUNK = 64
GDN_SUPER = 256
GDN_BASE = 8
GDN_GROUP = 4
CONV_K = 5
CONV_HALO = 8
XATTN_HEADS = 4
ROPE_BASE = 10000.0
NORM_EPS = 1e-6
LANES = 128
VMEM_LIMIT = 56 * 1024 * 1024


def _cparams(*sem):
    return pltpu.CompilerParams(dimension_semantics=sem, vmem_limit_bytes=VMEM_LIMIT)


def _rms(x, w):
    return x * lax.rsqrt(jnp.mean(x * x, axis=-1, keepdims=True) + NORM_EPS) * w


def _silu(x):
    return x * jax.nn.sigmoid(x)


def _dot(a, b):
    return jnp.dot(a.astype(BF16), b.astype(BF16), preferred_element_type=F32)


def _dot_nt(a, b):
    return lax.dot_general(a.astype(BF16), b.astype(BF16), (((1,), (1,)), ((), ())),
                           preferred_element_type=F32)


def _dot_tn(a, b):
    return lax.dot_general(a.astype(BF16), b.astype(BF16), (((0,), (0,)), ((), ())),
                           preferred_element_type=F32)


def _split2(a):
    hi = a.astype(BF16)
    lo = (a - hi.astype(F32)).astype(BF16)
    return hi, lo


def _split3(a):
    hi = a.astype(BF16)
    r = a - hi.astype(F32)
    mid = r.astype(BF16)
    lo = (r - mid.astype(F32)).astype(BF16)
    return hi, mid, lo


def _rope_kernel(pos_ref, invf_ref, sign_ref, cos_ref, sin_ref):
    ang = pos_ref[0].astype(F32) * invf_ref[...]
    cos_ref[0] = jnp.cos(ang)
    sin_ref[0] = jnp.sin(ang) * sign_ref[...]


def _rope_tables(positions):
    b, s = positions.shape
    ts = min(s, 512)
    half = HEAD_DIM // 2
    inv_freq = ROPE_BASE ** (-jnp.arange(0, HEAD_DIM, 2, dtype=F32) / HEAD_DIM)
    invf = jnp.concatenate([inv_freq, inv_freq])[None, :]
    sign = jnp.concatenate([-jnp.ones((half,), F32), jnp.ones((half,), F32)])[None, :]
    tab = jax.ShapeDtypeStruct((b, s, HEAD_DIM), F32)
    return pl.pallas_call(
        _rope_kernel,
        grid=(b, s // ts),
        in_specs=[pl.BlockSpec((1, ts, 1), lambda i, j: (i, j, 0)),
                  pl.BlockSpec((1, HEAD_DIM), lambda i, j: (0, 0)),
                  pl.BlockSpec((1, HEAD_DIM), lambda i, j: (0, 0))],
        out_specs=[pl.BlockSpec((1, ts, HEAD_DIM), lambda i, j: (i, j, 0)),
                   pl.BlockSpec((1, ts, HEAD_DIM), lambda i, j: (i, j, 0))],
        out_shape=[tab, tab],
        compiler_params=_cparams("parallel", "parallel"),
        name="rope_tables",
    )(positions.reshape(b, s, 1), invf, sign)


def _ffn_kernel(x_ref, npre_ref, wg_ref, wu_ref, wd_ref, npost_ref, o_ref):
    x = x_ref[...]
    u = _rms(x, npre_ref[...]).astype(BF16)
    g = jnp.dot(u, wg_ref[...], preferred_element_type=F32)
    up = jnp.dot(u, wu_ref[...], preferred_element_type=F32)
    h = (_silu(g) * up).astype(BF16)
    y = jnp.dot(h, wd_ref[...], preferred_element_type=F32)
    o_ref[...] = x + 0.5 * _rms(y, npost_ref[...])


def _resident(shape):
    return pl.BlockSpec(shape, lambda *_: (0,) * len(shape), pipeline_mode=pl.Buffered(1))


def _ffn(x2, npre, wg, wu, wd, npost, tm):
    t, d = x2.shape
    f = wg.shape[1]
    row = pl.BlockSpec((tm, d), lambda i: (i, 0))
    return pl.pallas_call(
        _ffn_kernel,
        grid=(t // tm,),
        in_specs=[row, _resident((1, d)), _resident((d, f)), _resident((d, f)),
                  _resident((f, d)), _resident((1, d))],
        out_specs=row,
        out_shape=jax.ShapeDtypeStruct((t, d), F32),
        compiler_params=_cparams("parallel"),
        name="ffn",
    )(x2, npre, wg, wu, wd, npost)


def _inproj_kernel(x_ref, npre_ref, w_ref, wab_hi_ref, wab_lo_ref, proj_ref, ab_ref):
    u = _rms(x_ref[...], npre_ref[...])
    u_hi, u_lo = _split2(u)
    proj_ref[...] = jnp.dot(u_hi, w_ref[...], preferred_element_type=F32)
    d = functools.partial(jnp.dot, preferred_element_type=F32)
    ab_ref[...] = d(u_hi, wab_hi_ref[...]) + (d(u_lo, wab_hi_ref[...]) + d(u_hi, wab_lo_ref[...]))


def _inproj(x2, npre, w_main, wab_hi, wab_lo, tm):
    t, d = x2.shape
    n = w_main.shape[1]
    return pl.pallas_call(
        _inproj_kernel,
        grid=(t // tm,),
        in_specs=[pl.BlockSpec((tm, d), lambda i: (i, 0)), _resident((1, d)), _resident((d, n)),
                  _resident((d, LANES)), _resident((d, LANES))],
        out_specs=[pl.BlockSpec((tm, n), lambda i: (i, 0)),
                   pl.BlockSpec((tm, LANES), lambda i: (i, 0))],
        out_shape=[jax.ShapeDtypeStruct((t, n), F32), jax.ShapeDtypeStruct((t, LANES), F32)],
        compiler_params=_cparams("parallel"),
        name="inproj",
    )(x2, npre, w_main, wab_hi, wab_lo)


def _gate_kernel(ab_ref, aexp_ref, dtb_ref, o_ref):
    s = ab_ref.shape[1]
    sup = GDN_SUPER
    grp = 2 * GDN_HEADS
    ri = lax.broadcasted_iota(jnp.int32, (sup, sup), 0)
    ci = lax.broadcasted_iota(jnp.int32, (sup, sup), 1)
    same = (ri // GDN_CHUNK) == (ci // GDN_CHUNK)
    tril = jnp.where(same & (ci <= ri), 1.0, 0.0).astype(BF16)
    triu = jnp.where(same & (ci >= ri), 1.0, 0.0).astype(BF16)
    ones = jnp.where(same, 1.0, 0.0).astype(BF16)
    col = lax.broadcasted_iota(jnp.int32, (sup, LANES), 1)
    aexp = aexp_ref[...]
    dtb = dtb_ref[...]
    d = functools.partial(jnp.dot, preferred_element_type=F32)

    def body(r, carry):
        t0 = pl.multiple_of(r * sup, sup)
        ab = ab_ref[0, pl.ds(t0, sup), :]
        xs = ab + dtb
        softplus = jnp.maximum(xs, 0.0) + jnp.log1p(jnp.exp(-jnp.abs(xs)))
        g = -aexp * softplus
        beta = jax.nn.sigmoid(ab)
        g_hi, g_mid, g_lo = _split3(g)
        pre = d(tril, g_hi) + (d(tril, g_mid) + d(tril, g_lo))
        suf = d(triu, g_hi) + (d(triu, g_mid) + d(triu, g_lo))
        tot = d(ones, g_hi) + (d(ones, g_mid) + d(ones, g_lo))
        gc = jnp.where(col < GDN_HEADS, pre, suf)
        eg = jnp.exp(gc)
        beg = beta * pltpu.roll(eg, grp, 1)
        groups = (gc, beta, pltpu.roll(eg, 2 * grp, 1), pltpu.roll(beg, 2 * grp, 1),
                  pltpu.roll(jnp.exp(tot - gc), 4 * grp, 1), pltpu.roll(jnp.exp(tot), 5 * grp, 1))
        out = groups[-1]
        for i in range(len(groups) - 2, -1, -1):
            out = jnp.where(col < (i + 1) * grp, groups[i], out)
        o_ref[0, pl.ds(t0, sup), :] = out
        return carry

    lax.fori_loop(0, s // sup, body, 0)


def _gates(ab, aexp_row, dtb_row):
    b, s, _ = ab.shape
    blk = pl.BlockSpec((1, s, LANES), lambda i: (i, 0, 0))
    vec = pl.BlockSpec((1, LANES), lambda i: (0, 0))
    return pl.pallas_call(
        _gate_kernel,
        grid=(b,),
        in_specs=[blk, vec, vec],
        out_specs=blk,
        out_shape=jax.ShapeDtypeStruct((b, s, LANES), F32),
        compiler_params=_cparams("parallel"),
        name="gdn_gates",
    )(ab, aexp_row, dtb_row)


def _retention_kernel(lgam_ref, rq_ref, rk_ref, rv_ref, rg_ref, cos_ref, sin_ref, o_ref,
                      q_s, k_s, kvf_s, stb_s, st_s):
    s = rq_ref.shape[1]
    c = RET_CHUNK
    n = s // c
    grp = RET_GROUP if n % RET_GROUP == 0 else 1
    h = pl.program_id(1)
    lg_f = lgam_ref[0, h]
    lg_b = lgam_ref[1, h]

    ri = lax.broadcasted_iota(jnp.int32, (c, c), 0)
    ci = lax.broadcasted_iota(jnp.int32, (c, c), 1)
    rel = (ri - ci).astype(F32)
    m_f = rel >= 0
    dmat = jnp.where(m_f, jnp.exp(jnp.where(m_f, rel, 0.0) * lg_f),
                     jnp.exp(jnp.where(m_f, 0.0, -rel) * lg_b))
    pos = lax.broadcasted_iota(jnp.int32, (c, HEAD_DIM), 0).astype(F32)
    qdec_f = jnp.exp((pos + 1.0) * lg_f)
    kdec_f = jnp.exp((c - 1.0 - pos) * lg_f)
    qdec_b = jnp.exp((c - pos) * lg_b)
    kdec_b = jnp.exp(pos * lg_b)
    cd_f = jnp.exp(c * lg_f)
    cd_b = jnp.exp(c * lg_b)

    def rows_of(chunk):
        return pl.ds(pl.multiple_of(chunk * c, c), c)

    st_s[...] = jnp.zeros_like(st_s)

    def pass1(i, carry):
        chunks = [n - 1 - (i * grp + g) for g in range(grp)]
        kvs = []
        for ch in chunks:
            sl = rows_of(ch)
            cs = cos_ref[0, sl, :]
            sn = sin_ref[0, sl, :]
            q = rq_ref[0, sl, :]
            k = rk_ref[0, sl, :]
            v = rv_ref[0, sl, :]
            q_s[sl, :] = (q * cs + pltpu.roll(q, HEAD_DIM // 2, 1) * sn).astype(BF16)
            kr = ((k * cs + pltpu.roll(k, HEAD_DIM // 2, 1) * sn) * (HEAD_DIM ** -0.5)).astype(BF16)
            k_s[sl, :] = kr
            kvs.append(_dot_tn(kr, jnp.concatenate([v * kdec_f, v * kdec_b], axis=-1)))
        st = st_s[...]
        for ch, kv in zip(chunks, kvs):
            kvf_s[ch] = kv[:, :HEAD_DIM]
            stb_s[ch] = st.astype(BF16)
            st = st * cd_b + kv[:, HEAD_DIM:]
        st_s[...] = st
        return carry

    lax.fori_loop(0, n // grp, pass1, 0)

    st_s[...] = jnp.zeros_like(st_s)

    def pass2(i, carry):
        chunks = [i * grp + g for g in range(grp)]
        st = st_s[...]
        sts = []
        for ch in chunks:
            sts.append(jnp.concatenate([st.astype(BF16), stb_s[ch]], axis=-1))
            st = st * cd_f + kvf_s[ch]
        st_s[...] = st
        qs = [q_s[rows_of(ch), :] for ch in chunks]
        scores = [(_dot_nt(q, k_s[rows_of(ch), :]) * dmat).astype(BF16) for q, ch in zip(qs, chunks)]
        inter = [jnp.dot(q, st2, preferred_element_type=F32) for q, st2 in zip(qs, sts)]
        for ch, sc, it in zip(chunks, scores, inter):
            sl = rows_of(ch)
            o = (jnp.dot(sc, rv_ref[0, sl, :].astype(BF16), preferred_element_type=F32)
                 + it[:, :HEAD_DIM] * qdec_f + it[:, HEAD_DIM:] * qdec_b)
            mu = jnp.mean(o, axis=-1, keepdims=True)
            oc = o - mu
            var = jnp.mean(oc * oc, axis=-1, keepdims=True)
            o_ref[0, sl, :] = _silu(rg_ref[0, sl, :]) * (oc * lax.rsqrt(var + NORM_EPS))
        return carry

    lax.fori_loop(0, n // grp, pass2, 0)


def _retention(proj, cosf, sinf, lgam):
    b, s, _ = proj.shape
    hb = lambda off: pl.BlockSpec((1, s, HEAD_DIM), lambda i, j, off=off: (i, 0, off + j))
    tab = pl.BlockSpec((1, s, HEAD_DIM), lambda i, j: (i, 0, 0))
    return pl.pallas_call(
        _retention_kernel,
        grid=(b, RET_HEADS),
        in_specs=[pl.BlockSpec(memory_space=pltpu.SMEM),
                  hb(0), hb(RET_HEADS), hb(2 * RET_HEADS), hb(3 * RET_HEADS), tab, tab],
        out_specs=pl.BlockSpec((1, s, HEAD_DIM), lambda i, j: (i, 0, j)),
        out_shape=jax.ShapeDtypeStruct((b, s, RET_HEADS * HEAD_DIM), F32),
        scratch_shapes=[pltpu.VMEM((s, HEAD_DIM), BF16), pltpu.VMEM((s, HEAD_DIM), BF16),
                        pltpu.VMEM((s // RET_CHUNK, HEAD_DIM, HEAD_DIM), F32),
                        pltpu.VMEM((s // RET_CHUNK, HEAD_DIM, HEAD_DIM), BF16),
                        pltpu.VMEM((HEAD_DIM, HEAD_DIM), F32)],
        compiler_params=_cparams("parallel", "parallel"),
        name="retention",
    )(lgam, proj, proj, proj, proj, cosf, sinf)


def _tile_rows(xs, reps):
    return jnp.concatenate([xs.astype(BF16)] * reps, axis=0)


def _unit_tri_inverses(ls, same, offs, base_s, eye_s):
    reps = same.shape[0] // eye_s.shape[0]
    dot = functools.partial(jnp.dot, preferred_element_type=F32)
    bd = lambda x: _tile_rows(x, reps) * same
    tiled = [_tile_rows(l, reps) for l in ls]
    xs = [l * base_s for l in ls]
    ps = [eye_s - x for x in xs]
    ws = [bd(x) for x in xs]
    for _ in range((GDN_BASE - 1).bit_length() - 1):
        xs = [dot(x.astype(BF16), w) for x, w in zip(xs, ws)]
        ws = [bd(x) for x in xs]
        ps = [p + dot(p.astype(BF16), w) for p, w in zip(ps, ws)]
    for off in offs:
        ys = [dot(p.astype(BF16), t * off) for p, t in zip(ps, tiled)]
        ws = [bd(p) for p in ps]
        ps = [p - dot(y.astype(BF16), w) for p, y, w in zip(ps, ys, ws)]
    return [bd(p) for p in ps]


def _gdn_kernel(q_ref, k_ref, v_ref, z_ref, wq_ref, wk_ref, wv_ref, gcol_ref, grow_ref, nw_ref,
                o_ref, xp_s, u_s, w_s, qg_s, kt_s, m2_s, ob_s, st_s):
    s = q_ref.shape[1]
    c = GDN_CHUNK
    n = s // c
    sup = GDN_SUPER
    nsup = s // sup
    halo = CONV_HALO
    left = (CONV_K - 1) // 2

    ri = lax.broadcasted_iota(jnp.int32, (sup, sup), 0)
    ci = lax.broadcasted_iota(jnp.int32, (sup, sup), 1)
    blk = lambda size: (ri // size) == (ci // size)
    same = jnp.where(blk(c), 1.0, 0.0).astype(BF16)
    offs = []
    size = GDN_BASE
    while size < c:
        offs.append(jnp.where(blk(2 * size) & ~blk(size), 1.0, 0.0).astype(BF16))
        size *= 2
    si = lax.broadcasted_iota(jnp.int32, (c, sup), 0)
    sm = lax.broadcasted_iota(jnp.int32, (c, sup), 1) % c
    eye_s = jnp.where(si == sm, 1.0, 0.0)
    base_s = jnp.where(si // GDN_BASE == sm // GDN_BASE, 1.0, 0.0)
    ti = lax.broadcasted_iota(jnp.int32, (sup, LANES), 0)
    li = lax.broadcasted_iota(jnp.int32, (sup, LANES), 1)
    mi = li - ((ti // c) % (LANES // c)) * c
    own = (mi >= 0) & (mi < c)
    ii = ti % c
    incl = (jnp.where(own & (mi <= ii), 1.0, 0.0), jnp.where(own & (mi >= ii), 1.0, 0.0))
    strict = (jnp.where(own & (mi < ii), 1.0, 0.0), jnp.where(own & (mi > ii), 1.0, 0.0))
    halves = sup // LANES

    def halo_copy(r, carry):
        t0 = pl.multiple_of(r * sup, sup)
        for t, raw in enumerate((q_ref, k_ref, v_ref)):
            xp_s[t, pl.ds(t0 + halo, sup), :] = raw[0, pl.ds(t0, sup), :]
        return carry

    for t in range(3):
        xp_s[t, 0:halo, :] = jnp.zeros((halo, HEAD_DIM), F32)
        xp_s[t, halo + s:halo + s + halo, :] = jnp.zeros((halo, HEAD_DIM), F32)
    lax.fori_loop(0, nsup, halo_copy, 0)

    def conv_tile(t, w_ref, t0):
        w = w_ref[...]
        acc = xp_s[t, pl.ds(t0 + halo - left, sup), :] * w[0:1, :]
        for j in range(1, CONV_K):
            acc = acc + xp_s[t, pl.ds(t0 + halo - left + j, sup), :] * w[j:j + 1, :]
        return _silu(acc)

    def l2n(y):
        return y * lax.rsqrt(jnp.sum(y * y, axis=-1, keepdims=True) + NORM_EPS)

    def own_tiles(x):
        return jnp.concatenate(x, axis=0)

    def prep(r):
        t0 = pl.multiple_of(r * sup, sup)
        rows = pl.ds(t0, sup)
        q = l2n(conv_tile(0, wq_ref, t0)) * (HEAD_DIM ** -0.5)
        k = l2n(conv_tile(1, wk_ref, t0))
        v = conv_tile(2, wv_ref, t0)
        grams = []
        for p in range(halves):
            hs = slice(p * LANES, (p + 1) * LANES)
            grams.append(_dot_nt(jnp.concatenate([q[hs], k[hs]], axis=0), k[hs]))
        qk = own_tiles([g[:LANES] for g in grams])
        kk = own_tiles([g[LANES:] for g in grams])
        gates = gcol_ref[rows, :]
        grw = grow_ref[r]
        gcol = lambda j, d: gates[:, 2 * j + d:2 * j + d + 1]
        ls = []
        for d in range(2):
            diff = own_tiles([gcol(0, d)[p * LANES:(p + 1) * LANES] - grw[d:d + 1, p * LANES:(p + 1) * LANES]
                              for p in range(halves)])
            e = jnp.exp(jnp.minimum(diff, 0.0))
            m2_s[d, rows, :] = (qk * (e * incl[d])).astype(BF16)
            lt = kk * (e * strict[d]) * gcol(1, d)
            ls.append(jnp.concatenate(
                [sum(lt[p * LANES + j * c:p * LANES + (j + 1) * c] for j in range(LANES // c))
                 for p in range(halves)], axis=-1))
        return rows, q, k, v, gcol, ls

    def finish(rows, q, k, v, gcol, tbds):
        for d in range(2):
            rhs = jnp.concatenate([v * gcol(1, d), k * gcol(3, d)], axis=-1).astype(BF16)
            sol = jnp.dot(tbds[d], rhs, preferred_element_type=F32)
            u_s[d, rows, :] = sol[:, :HEAD_DIM].astype(BF16)
            w_s[d, rows, :] = sol[:, HEAD_DIM:].astype(BF16)
            qg_s[d, rows, :] = (q * gcol(2, d)).astype(BF16)
            kt_s[d, rows, :] = (k * gcol(4, d)).astype(BF16)

    group = GDN_GROUP if nsup % GDN_GROUP == 0 else 1

    def phase_a(i, carry):
        preps = [prep(i * group + g) for g in range(group)]
        tbds = _unit_tri_inverses([l for pr in preps for l in pr[5]], same, offs, base_s, eye_s)
        for g, pr in enumerate(preps):
            finish(*pr[:5], tbds[2 * g:2 * g + 2])
        return carry

    lax.fori_loop(0, nsup // group, phase_a, 0)

    st_s[...] = jnp.zeros_like(st_s)

    outs = (o_ref.at[0], ob_s)
    dot = functools.partial(jnp.dot, preferred_element_type=F32)

    def phase_b(i, carry):
        t0s = (pl.multiple_of(i * c, c), pl.multiple_of((n - 1 - i) * c, c))
        rows = [pl.ds(t0, c) for t0 in t0s]
        sts = [st_s[d] for d in range(2)]
        both = [dot(jnp.concatenate([w_s[d, rows[d], :], qg_s[d, rows[d], :]], axis=0),
                    sts[d].astype(BF16)) for d in range(2)]
        vbs = [(u_s[d, rows[d], :].astype(F32) - both[d][:c]).astype(BF16) for d in range(2)]
        for d in range(2):
            vv = jnp.concatenate([vbs[d], vbs[d]], axis=0)
            outs[d][rows[d], :] = both[d][c:] + dot(m2_s[d, rows[d], :], vv)
        for d in range(2):
            decay = gcol_ref[pl.ds(t0s[d], 1), 10 + d:11 + d]
            st_s[d] = sts[d] * decay + _dot_tn(kt_s[d, rows[d], :], vbs[d])
        return carry

    lax.fori_loop(0, n, phase_b, 0, unroll=2)

    nw = nw_ref[...]

    def fin(r, carry):
        sl = pl.ds(pl.multiple_of(r * sup, sup), sup)
        o = o_ref[0, sl, :] + ob_s[sl, :]
        o = o * lax.rsqrt(jnp.mean(o * o, axis=-1, keepdims=True) + NORM_EPS) * nw
        o_ref[0, sl, :] = o * _silu(z_ref[0, sl, :])
        return carry

    lax.fori_loop(0, nsup, fin, 0)


def _gdn(proj, conv_w, gcol, grow, norm_w):
    b, s, _ = proj.shape
    assert s % GDN_SUPER == 0
    base = 4 * RET_HEADS
    hb = lambda off: pl.BlockSpec((1, s, HEAD_DIM), lambda i, j, off=off: (i, 0, off + j))
    cw = lambda off: pl.BlockSpec((CONV_K, HEAD_DIM), lambda i, j, off=off: (0, off + j))
    seq2 = pltpu.VMEM((2, s, HEAD_DIM), BF16)
    return pl.pallas_call(
        _gdn_kernel,
        grid=(b, GDN_HEADS),
        in_specs=[hb(base), hb(base + GDN_HEADS), hb(base + 2 * GDN_HEADS), hb(base + 3 * GDN_HEADS),
                  cw(0), cw(GDN_HEADS), cw(2 * GDN_HEADS),
                  pl.BlockSpec((None, None, s, gcol.shape[-1]), lambda i, j: (i, j, 0, 0)),
                  pl.BlockSpec((None, None, s // GDN_SUPER, 2, GDN_SUPER), lambda i, j: (i, j, 0, 0, 0)),
                  pl.BlockSpec((1, HEAD_DIM), lambda i, j: (0, 0))],
        out_specs=pl.BlockSpec((1, s, HEAD_DIM), lambda i, j: (i, 0, j)),
        out_shape=jax.ShapeDtypeStruct((b, s, GDN_HEADS * HEAD_DIM), F32),
        scratch_shapes=[pltpu.VMEM((3, s + 2 * CONV_HALO, HEAD_DIM), F32),
                        seq2, seq2, seq2, seq2, seq2, pltpu.VMEM((s, HEAD_DIM), F32),
                        pltpu.VMEM((2, HEAD_DIM, HEAD_DIM), F32)],
        compiler_params=_cparams("parallel", "parallel"),
        name="gdn",
    )(proj, proj, proj, proj, conv_w, conv_w, conv_w, gcol, grow, norm_w)


def _outproj_kernel(x_ref, oret_ref, ogdn_ref, wa_ref, wb_ref, npost_ref, o_ref):
    h = _dot(oret_ref[...], wa_ref[...]) + _dot(ogdn_ref[...], wb_ref[...])
    o_ref[...] = x_ref[...] + _rms(h, npost_ref[...])


def _outproj(x2, o_ret, o_gdn, w_a, w_b, npost, tm):
    t, d = x2.shape
    wr = o_ret.shape[1]
    wg = o_gdn.shape[1]
    row = pl.BlockSpec((tm, d), lambda i: (i, 0))
    return pl.pallas_call(
        _outproj_kernel,
        grid=(t // tm,),
        in_specs=[row, pl.BlockSpec((tm, wr), lambda i: (i, 0)), pl.BlockSpec((tm, wg), lambda i: (i, 0)),
                  _resident((wr, d)), _resident((wg, d)), _resident((1, d))],
        out_specs=row,
        out_shape=jax.ShapeDtypeStruct((t, d), F32),
        compiler_params=_cparams("parallel"),
        name="outproj",
    )(x2, o_ret, o_gdn, w_a, w_b, npost)


def _memkv_kernel(mem_ref, mw_ref, wk_ref, wv_ref, k_ref, v_ref):
    m = _rms(mem_ref[0], mw_ref[...]).astype(BF16)
    k_ref[0] = jnp.dot(m, wk_ref[...], preferred_element_type=F32).astype(BF16)
    v_ref[0] = jnp.dot(m, wv_ref[...], preferred_element_type=F32).astype(BF16)


def _memkv(mem, mw, wk, wv):
    b, m, d = mem.shape
    blk = pl.BlockSpec((1, m, d), lambda i: (i, 0, 0))
    out = jax.ShapeDtypeStruct((b, m, d), BF16)
    return pl.pallas_call(
        _memkv_kernel,
        grid=(b,),
        in_specs=[blk, _resident((1, d)), _resident((d, d)), _resident((d, d))],
        out_specs=[blk, blk],
        out_shape=[out, out],
        compiler_params=_cparams("parallel"),
        name="mem_kv",
    )(mem, mw, wk, wv)


def _xattn_kernel(x_ref, npre_ref, wq_ref, k_ref, v_ref, wo_ref, npost_ref, o_ref):
    x = x_ref[0]
    d = x.shape[-1]
    hd = d // XATTN_HEADS
    u = _rms(x, npre_ref[...]).astype(BF16)
    q = jnp.dot(u, wq_ref[...], preferred_element_type=F32).astype(BF16)
    heads = []
    for h in range(XATTN_HEADS):
        sl = slice(h * hd, (h + 1) * hd)
        sc = lax.dot_general(q[:, sl], k_ref[0, :, sl], (((1,), (1,)), ((), ())),
                             preferred_element_type=F32) * (hd ** -0.5)
        e = jnp.exp(sc - jnp.max(sc, axis=-1, keepdims=True))
        p = e / jnp.sum(e, axis=-1, keepdims=True)
        heads.append(jnp.dot(p.astype(BF16), v_ref[0, :, sl], preferred_element_type=F32))
    o = jnp.concatenate(heads, axis=-1).astype(BF16)
    hh = jnp.dot(o, wo_ref[...], preferred_element_type=F32)
    o_ref[0] = x + _rms(hh, npost_ref[...])


def _xattn(x, npre, wq, kmem, vmem, wo, npost, tm):
    b, s, d = x.shape
    m = kmem.shape[1]
    row = pl.BlockSpec((1, tm, d), lambda i, j: (i, j, 0))
    kv = pl.BlockSpec((1, m, d), lambda i, j: (i, 0, 0))
    return pl.pallas_call(
        _xattn_kernel,
        grid=(b, s // tm),
        in_specs=[row, _resident((1, d)), _resident((d, d)), kv, kv, _resident((d, d)),
                  _resident((1, d))],
        out_specs=row,
        out_shape=jax.ShapeDtypeStruct((b, s, d), F32),
        compiler_params=_cparams("parallel", "parallel"),
        name="xattn",
    )(x, npre, wq, kmem, vmem, wo, npost)


def _pad_lanes(v):
    return jnp.pad(v.reshape(1, -1).astype(F32), ((0, 0), (0, LANES - v.size)))


def kernel(x, mem, positions, norm_pre, norm_post, mem_norm, ffn1_gate, ffn1_up, ffn1_down, w_in, gdn_conv, ret_log_gamma, gdn_a_log, gdn_dt_bias, gdn_norm, w_out, xattn_q, xattn_k, xattn_v, xattn_o, ffn2_gate, ffn2_up, ffn2_down):
    b, s, d = x.shape
    depth = norm_pre.shape[0]
    t = b * s
    tm = 512 if s % 512 == 0 else s
    ret_w = RET_HEADS * HEAD_DIM
    main_cols = 4 * ret_w + 4 * GDN_HEADS * HEAD_DIM
    bf = lambda w: w.astype(BF16)
    vec = lambda w: w.reshape(1, -1).astype(F32)

    cosf, sinf = _rope_tables(positions)
    x2 = x.reshape(t, d)
    for l in range(depth):
        x2 = _ffn(x2, vec(norm_pre[l, 0]), bf(ffn1_gate[l]), bf(ffn1_up[l]), bf(ffn1_down[l]),
                  vec(norm_post[l, 0]), tm)

        wab = jnp.pad(w_in[l][:, main_cols:], ((0, 0), (0, LANES - 4 * GDN_HEADS)))
        wab_hi = wab.astype(BF16)
        wab_lo = (wab - wab_hi.astype(F32)).astype(BF16)
        proj, ab = _inproj(x2, vec(norm_pre[l, 1]), bf(w_in[l][:, :main_cols]), wab_hi, wab_lo, tm)
        proj = proj.reshape(b, s, main_cols)
        gates = _gates(ab.reshape(b, s, LANES), _pad_lanes(jnp.exp(gdn_a_log[l].astype(F32))),
                       _pad_lanes(gdn_dt_bias[l]))
        ngrp = 6
        gcol = gates[:, :, :ngrp * 2 * GDN_HEADS].reshape(b, s, ngrp * 2, GDN_HEADS).transpose(0, 3, 1, 2)
        grow = gates[:, :, :2 * GDN_HEADS].reshape(b, s // GDN_SUPER, GDN_SUPER, 2, GDN_HEADS)
        grow = grow.transpose(0, 4, 1, 3, 2)
        o_ret = _retention(proj, cosf, sinf, ret_log_gamma[l].astype(F32))
        o_gdn = _gdn(proj, gdn_conv[l].astype(F32), gcol, grow, vec(gdn_norm[l]))
        x2 = _outproj(x2, o_ret.reshape(t, ret_w), o_gdn.reshape(t, -1),
                      bf(w_out[l][:ret_w]), bf(w_out[l][ret_w:]), vec(norm_post[l, 1]), tm)

        kmem, vmem = _memkv(mem, vec(mem_norm[l]), bf(xattn_k[l]), bf(xattn_v[l]))
        x3 = _xattn(x2.reshape(b, s, d), vec(norm_pre[l, 2]), bf(xattn_q[l]), kmem, vmem,
                    bf(xattn_o[l]), vec(norm_post[l, 2]), tm)
        x2 = x3.reshape(t, d)

        x2 = _ffn(x2, vec(norm_pre[l, 3]), bf(ffn2_gate[l]), bf(ffn2_up[l]), bf(ffn2_down[l]),
                  vec(norm_post[l, 3]), tm)
    return x2.reshape(b, s, d)
```

```python
import functools

import jax
import jax.numpy as jnp
from jax import lax
from jax.experimental import pallas as pl
from jax.experimental.pallas import tpu as pltpu

F32 = jnp.float32
BF16 = jnp.bfloat16

HEAD_DIM = 128
RET_HEADS = 4
RET_CHUNK = 128
RET_GROUP = 4
GDN_HEADS = 4
GDN_CHUNK = 64
GDN_SUPER = 256
GDN_BASE = 8
GDN_GROUP = 4
CONV_K = 5
CONV_HALO = 8
XATTN_HEADS = 4
ROPE_BASE = 10000.0
NORM_EPS = 1e-6
LANES = 128
VMEM_LIMIT = 56 * 1024 * 1024


def _cparams(*sem):
    return pltpu.CompilerParams(dimension_semantics=sem, vmem_limit_bytes=VMEM_LIMIT)


def _rms(x, w):
    return x * lax.rsqrt(jnp.mean(x * x, axis=-1, keepdims=True) + NORM_EPS) * w


def _silu(x):
    return x * jax.nn.sigmoid(x)


def _dot(a, b):
    return jnp.dot(a.astype(BF16), b.astype(BF16), preferred_element_type=F32)


def _dot_nt(a, b):
    return lax.dot_general(a.astype(BF16), b.astype(BF16), (((1,), (1,)), ((), ())),
                           preferred_element_type=F32)


def _dot_tn(a, b):
    return lax.dot_general(a.astype(BF16), b.astype(BF16), (((0,), (0,)), ((), ())),
                           preferred_element_type=F32)


def _split2(a):
    hi = a.astype(BF16)
    lo = (a - hi.astype(F32)).astype(BF16)
    return hi, lo


def _split3(a):
    hi = a.astype(BF16)
    r = a - hi.astype(F32)
    mid = r.astype(BF16)
    lo = (r - mid.astype(F32)).astype(BF16)
    return hi, mid, lo


def _rope_kernel(pos_ref, invf_ref, sign_ref, cos_ref, sin_ref):
    ang = pos_ref[0].astype(F32) * invf_ref[...]
    cos_ref[0] = jnp.cos(ang)
    sin_ref[0] = jnp.sin(ang) * sign_ref[...]


def _rope_tables(positions):
    b, s = positions.shape
    ts = min(s, 512)
    half = HEAD_DIM // 2
    inv_freq = ROPE_BASE ** (-jnp.arange(0, HEAD_DIM, 2, dtype=F32) / HEAD_DIM)
    invf = jnp.concatenate([inv_freq, inv_freq])[None, :]
    sign = jnp.concatenate([-jnp.ones((half,), F32), jnp.ones((half,), F32)])[None, :]
    tab = jax.ShapeDtypeStruct((b, s, HEAD_DIM), F32)
    return pl.pallas_call(
        _rope_kernel,
        grid=(b, s // ts),
        in_specs=[pl.BlockSpec((1, ts, 1), lambda i, j: (i, j, 0)),
                  pl.BlockSpec((1, HEAD_DIM), lambda i, j: (0, 0)),
                  pl.BlockSpec((1, HEAD_DIM), lambda i, j: (0, 0))],
        out_specs=[pl.BlockSpec((1, ts, HEAD_DIM), lambda i, j: (i, j, 0)),
                   pl.BlockSpec((1, ts, HEAD_DIM), lambda i, j: (i, j, 0))],
        out_shape=[tab, tab],
        compiler_params=_cparams("parallel", "parallel"),
        name="rope_tables",
    )(positions.reshape(b, s, 1), invf, sign)


def _ffn_kernel(x_ref, npre_ref, wg_ref, wu_ref, wd_ref, npost_ref, o_ref):
    x = x_ref[...]
    u = _rms(x, npre_ref[...]).astype(BF16)
    g = jnp.dot(u, wg_ref[...], preferred_element_type=F32)
    up = jnp.dot(u, wu_ref[...], preferred_element_type=F32)
    h = (_silu(g) * up).astype(BF16)
    y = jnp.dot(h, wd_ref[...], preferred_element_type=F32)
    o_ref[...] = x + 0.5 * _rms(y, npost_ref[...])


def _resident(shape):
    return pl.BlockSpec(shape, lambda *_: (0,) * len(shape), pipeline_mode=pl.Buffered(1))


def _ffn(x2, npre, wg, wu, wd, npost, tm):
    t, d = x2.shape
    f = wg.shape[1]
    row = pl.BlockSpec((tm, d), lambda i: (i, 0))
    return pl.pallas_call(
        _ffn_kernel,
        grid=(t // tm,),
        in_specs=[row, _resident((1, d)), _resident((d, f)), _resident((d, f)),
                  _resident((f, d)), _resident((1, d))],
        out_specs=row,
        out_shape=jax.ShapeDtypeStruct((t, d), F32),
        compiler_params=_cparams("parallel"),
        name="ffn",
    )(x2, npre, wg, wu, wd, npost)


def _inproj_kernel(x_ref, xprev_ref, xnext_ref, npre_ref, w_ref, wab_hi_ref, wab_lo_ref, cw_ref,
                   proj_ref, ab_ref, ext_s, *, tiles_per_seq):
    tm = x_ref.shape[0]
    halo = CONV_HALO
    left = (CONV_K - 1) // 2
    q0 = 4 * RET_HEADS * HEAD_DIM
    qw = 3 * GDN_HEADS * HEAD_DIM
    npre = npre_ref[...]
    u = _rms(x_ref[...], npre)
    u_hi, u_lo = _split2(u)
    d = functools.partial(jnp.dot, preferred_element_type=F32)

    i = pl.program_id(0) % tiles_per_seq
    u_ext = jnp.concatenate([_rms(xprev_ref[...], npre).astype(BF16), u_hi,
                             _rms(xnext_ref[...], npre).astype(BF16)], axis=0)
    ext_s[...] = d(u_ext, w_ref[:, q0:q0 + qw])
    zeros = jnp.zeros((halo, qw), F32)
    ext_s[0:halo, :] = jnp.where(i == 0, zeros, ext_s[0:halo, :])
    ext_s[halo + tm:, :] = jnp.where(i == tiles_per_seq - 1, zeros, ext_s[halo + tm:, :])

    ab_ref[...] = d(u_hi, wab_hi_ref[...]) + (d(u_lo, wab_hi_ref[...]) + d(u_hi, wab_lo_ref[...]))

    wide = 2 * LANES
    other = list(range(0, q0, wide)) + list(range(q0 + qw, proj_ref.shape[1], wide))
    nheads = qw // HEAD_DIM
    for col in range(0, qw, HEAD_DIM):
        hidx = col // HEAD_DIM
        todo = other[hidx:hidx + 1] if hidx + 1 < nheads else other[hidx:]
        for oc in todo:
            proj_ref[:, oc:oc + wide] = d(u_hi, w_ref[:, oc:oc + wide])
        cs = slice(col, col + HEAD_DIM)
        w = cw_ref[:, cs]
        acc = ext_s[halo - left:halo - left + tm, cs] * w[0:1, :]
        for j in range(1, CONV_K):
            acc = acc + ext_s[halo - left + j:halo - left + j + tm, cs] * w[j:j + 1, :]
        y = _silu(acc)
        if col < 2 * GDN_HEADS * HEAD_DIM:
            y = y * lax.rsqrt(jnp.sum(y * y, axis=-1, keepdims=True) + NORM_EPS)
            if col < GDN_HEADS * HEAD_DIM:
                y = y * (HEAD_DIM ** -0.5)
        proj_ref[:, q0 + col:q0 + col + HEAD_DIM] = y


def _inproj(x2, npre, w_main, wab_hi, wab_lo, conv_w, tm, s):
    t, d = x2.shape
    n = w_main.shape[1]
    hb = tm // CONV_HALO
    last = t // CONV_HALO - 1
    return pl.pallas_call(
        functools.partial(_inproj_kernel, tiles_per_seq=s // tm),
        grid=(t // tm,),
        in_specs=[pl.BlockSpec((tm, d), lambda i: (i, 0)),
                  pl.BlockSpec((CONV_HALO, d), lambda i: (jnp.maximum(i * hb - 1, 0), 0)),
                  pl.BlockSpec((CONV_HALO, d), lambda i: (jnp.minimum((i + 1) * hb, last), 0)),
                  _resident((1, d)), _resident((d, n)), _resident((d, LANES)), _resident((d, LANES)),
                  _resident(conv_w.shape)],
        out_specs=[pl.BlockSpec((tm, n), lambda i: (i, 0)),
                   pl.BlockSpec((tm, LANES), lambda i: (i, 0))],
        out_shape=[jax.ShapeDtypeStruct((t, n), F32), jax.ShapeDtypeStruct((t, LANES), F32)],
        scratch_shapes=[pltpu.VMEM((tm + 2 * CONV_HALO, 3 * GDN_HEADS * HEAD_DIM), F32)],
        compiler_params=_cparams("parallel"),
        name="inproj",
    )(x2, x2, x2, npre, w_main, wab_hi, wab_lo, conv_w)


def _gate_kernel(ab_ref, aexp_ref, dtb_ref, o_ref):
    s = ab_ref.shape[1]
    sup = GDN_SUPER
    grp = 2 * GDN_HEADS
    ri = lax.broadcasted_iota(jnp.int32, (sup, sup), 0)
    ci = lax.broadcasted_iota(jnp.int32, (sup, sup), 1)
    same = (ri // GDN_CHUNK) == (ci // GDN_CHUNK)
    tril = jnp.where(same & (ci <= ri), 1.0, 0.0).astype(BF16)
    triu = jnp.where(same & (ci >= ri), 1.0, 0.0).astype(BF16)
    col = lax.broadcasted_iota(jnp.int32, (sup, LANES), 1)
    aexp = aexp_ref[...]
    dtb = dtb_ref[...]
    d = functools.partial(jnp.dot, preferred_element_type=F32)

    def body(r, carry):
        t0 = pl.multiple_of(r * sup, sup)
        ab = ab_ref[0, pl.ds(t0, sup), :]
        xs = ab + dtb
        softplus = jnp.maximum(xs, 0.0) + jnp.log1p(jnp.exp(-jnp.abs(xs)))
        g = -aexp * softplus
        beta = jax.nn.sigmoid(ab)
        g_hi, g_mid, g_lo = _split3(g)
        pre = d(tril, g_hi) + (d(tril, g_mid) + d(tril, g_lo))
        suf = d(triu, g_hi) + (d(triu, g_mid) + d(triu, g_lo))
        tot = pre + (suf - g)
        gc = jnp.where(col < GDN_HEADS, pre, suf)
        eg = jnp.exp(gc)
        beg = beta * pltpu.roll(eg, grp, 1)
        groups = (gc, beta, pltpu.roll(eg, 2 * grp, 1), pltpu.roll(beg, 2 * grp, 1),
                  pltpu.roll(jnp.exp(tot - gc), 4 * grp, 1), pltpu.roll(jnp.exp(tot), 5 * grp, 1))
        out = groups[-1]
        for i in range(len(groups) - 2, -1, -1):
            out = jnp.where(col < (i + 1) * grp, groups[i], out)
        o_ref[0, pl.ds(t0, sup), :] = out
        return carry

    lax.fori_loop(0, s // sup, body, 0)


def _gates(ab, aexp_row, dtb_row):
    b, s, _ = ab.shape
    blk = pl.BlockSpec((1, s, LANES), lambda i: (i, 0, 0))
    vec = pl.BlockSpec((1, LANES), lambda i: (0, 0))
    return pl.pallas_call(
        _gate_kernel,
        grid=(b,),
        in_specs=[blk, vec, vec],
        out_specs=blk,
        out_shape=jax.ShapeDtypeStruct((b, s, LANES), F32),
        compiler_params=_cparams("parallel"),
        name="gdn_gates",
    )(ab, aexp_row, dtb_row)


def _retention_kernel(lgam_ref, rq_ref, rk_ref, rv_ref, rg_ref, cos_ref, sin_ref, o_ref,
                      q_s, k_s, kvf_s, stb_s, st_s):
    s = rq_ref.shape[1]
    c = RET_CHUNK
    n = s // c
    grp = RET_GROUP if n % RET_GROUP == 0 else 1
    h = pl.program_id(1)
    lg_f = lgam_ref[0, h]
    lg_b = lgam_ref[1, h]

    ri = lax.broadcasted_iota(jnp.int32, (c, c), 0)
    ci = lax.broadcasted_iota(jnp.int32, (c, c), 1)
    rel = (ri - ci).astype(F32)
    m_f = rel >= 0
    dmat = jnp.where(m_f, jnp.exp(jnp.where(m_f, rel, 0.0) * lg_f),
                     jnp.exp(jnp.where(m_f, 0.0, -rel) * lg_b))
    pos = lax.broadcasted_iota(jnp.int32, (c, HEAD_DIM), 0).astype(F32)
    qdec_f = jnp.exp((pos + 1.0) * lg_f)
    kdec_f = jnp.exp((c - 1.0 - pos) * lg_f)
    qdec_b = jnp.exp((c - pos) * lg_b)
    kdec_b = jnp.exp(pos * lg_b)
    cd_f = jnp.exp(c * lg_f)
    cd_b = jnp.exp(c * lg_b)

    def rows_of(chunk):
        return pl.ds(pl.multiple_of(chunk * c, c), c)

    st_s[...] = jnp.zeros_like(st_s)

    def pass1(i, carry):
        chunks = [n - 1 - (i * grp + g) for g in range(grp)]
        kvs = []
        for ch in chunks:
            sl = rows_of(ch)
            cs = cos_ref[0, sl, :]
            sn = sin_ref[0, sl, :]
            q = rq_ref[0, sl, :]
            k = rk_ref[0, sl, :]
            v = rv_ref[0, sl, :]
            q_s[sl, :] = (q * cs + pltpu.roll(q, HEAD_DIM // 2, 1) * sn).astype(BF16)
            kr = ((k * cs + pltpu.roll(k, HEAD_DIM // 2, 1) * sn) * (HEAD_DIM ** -0.5)).astype(BF16)
            k_s[sl, :] = kr
            kvs.append(_dot_tn(kr, jnp.concatenate([v * kdec_f, v * kdec_b], axis=-1)))
        st = st_s[...]
        for ch, kv in zip(chunks, kvs):
            kvf_s[ch] = kv[:, :HEAD_DIM]
            stb_s[ch] = st.astype(BF16)
            st = st * cd_b + kv[:, HEAD_DIM:]
        st_s[...] = st
        return carry

    lax.fori_loop(0, n // grp, pass1, 0)

    st_s[...] = jnp.zeros_like(st_s)

    def pass2(i, carry):
        chunks = [i * grp + g for g in range(grp)]
        st = st_s[...]
        sts = []
        for ch in chunks:
            sts.append(jnp.concatenate([st.astype(BF16), stb_s[ch]], axis=-1))
            st = st * cd_f + kvf_s[ch]
        st_s[...] = st
        qs = [q_s[rows_of(ch), :] for ch in chunks]
        scores = [(_dot_nt(q, k_s[rows_of(ch), :]) * dmat).astype(BF16) for q, ch in zip(qs, chunks)]
        inter = [jnp.dot(q, st2, preferred_element_type=F32) for q, st2 in zip(qs, sts)]
        for ch, sc, it in zip(chunks, scores, inter):
            sl = rows_of(ch)
            o = (jnp.dot(sc, rv_ref[0, sl, :].astype(BF16), preferred_element_type=F32)
                 + it[:, :HEAD_DIM] * qdec_f + it[:, HEAD_DIM:] * qdec_b)
            mu = jnp.mean(o, axis=-1, keepdims=True)
            oc = o - mu
            var = jnp.mean(oc * oc, axis=-1, keepdims=True)
            o_ref[0, sl, :] = (_silu(rg_ref[0, sl, :]) * (oc * lax.rsqrt(var + NORM_EPS))).astype(BF16)
        return carry

    lax.fori_loop(0, n // grp, pass2, 0)


def _retention(proj, cosf, sinf, lgam):
    b, s, _ = proj.shape
    hb = lambda off: pl.BlockSpec((1, s, HEAD_DIM), lambda i, j, off=off: (i, 0, off + j))
    tab = pl.BlockSpec((1, s, HEAD_DIM), lambda i, j: (i, 0, 0))
    return pl.pallas_call(
        _retention_kernel,
        grid=(b, RET_HEADS),
        in_specs=[pl.BlockSpec(memory_space=pltpu.SMEM),
                  hb(0), hb(RET_HEADS), hb(2 * RET_HEADS), hb(3 * RET_HEADS), tab, tab],
        out_specs=pl.BlockSpec((1, s, HEAD_DIM), lambda i, j: (i, 0, j)),
        out_shape=jax.ShapeDtypeStruct((b, s, RET_HEADS * HEAD_DIM), BF16),
        scratch_shapes=[pltpu.VMEM((s, HEAD_DIM), BF16), pltpu.VMEM((s, HEAD_DIM), BF16),
                        pltpu.VMEM((s // RET_CHUNK, HEAD_DIM, HEAD_DIM), F32),
                        pltpu.VMEM((s // RET_CHUNK, HEAD_DIM, HEAD_DIM), BF16),
                        pltpu.VMEM((HEAD_DIM, HEAD_DIM), F32)],
        compiler_params=_cparams("parallel", "parallel"),
        name="retention",
    )(lgam, proj, proj, proj, proj, cosf, sinf)


def _tile_rows(xs, reps):
    return jnp.concatenate([xs.astype(BF16)] * reps, axis=0)


def _unit_tri_inverses(ls, same, offs, base_s, eye_s):
    reps = same.shape[0] // eye_s.shape[0]
    dot = functools.partial(jnp.dot, preferred_element_type=F32)
    bd = lambda x: _tile_rows(x, reps) * same
    tiled = [_tile_rows(l, reps) for l in ls]
    c = eye_s.shape[0]
    xs = [l * base_s for l in ls]
    ps = [eye_s - x for x in xs]
    levels = (GDN_BASE - 1).bit_length() - 1
    xs = [dot(x.astype(BF16), bd(x)) for x in xs]
    for level in range(levels):
        ws = [bd(x) for x in xs]
        if level + 1 < levels:
            both = [dot(jnp.concatenate([p, x], axis=0).astype(BF16), w) for p, x, w in zip(ps, xs, ws)]
            ps = [p + b[:c] for p, b in zip(ps, both)]
            xs = [b[c:] for b in both]
        else:
            ps = [p + dot(p.astype(BF16), w) for p, w in zip(ps, ws)]
    for off in offs:
        ys = [dot(p.astype(BF16), t * off) for p, t in zip(ps, tiled)]
        ws = [bd(p) for p in ps]
        ps = [p - dot(y.astype(BF16), w) for p, y, w in zip(ps, ys, ws)]
    return [bd(p) for p in ps]


def _gdn_kernel(q_ref, k_ref, v_ref, z_ref, gcol_ref, grow_ref, nw_ref,
                o_ref, u_s, w_s, qg_s, kt_s, m2_s, of_s, ob_s, st_s):
    s = q_ref.shape[1]
    c = GDN_CHUNK
    n = s // c
    sup = GDN_SUPER
    nsup = s // sup

    ri = lax.broadcasted_iota(jnp.int32, (sup, sup), 0)
    ci = lax.broadcasted_iota(jnp.int32, (sup, sup), 1)
    blk = lambda size: (ri // size) == (ci // size)
    same = jnp.where(blk(c), 1.0, 0.0).astype(BF16)
    offs = []
    size = GDN_BASE
    while size < c:
        offs.append(jnp.where(blk(2 * size) & ~blk(size), 1.0, 0.0).astype(BF16))
        size *= 2
    si = lax.broadcasted_iota(jnp.int32, (c, sup), 0)
    sm = lax.broadcasted_iota(jnp.int32, (c, sup), 1) % c
    eye_s = jnp.where(si == sm, 1.0, 0.0)
    base_s = jnp.where(si // GDN_BASE == sm // GDN_BASE, 1.0, 0.0)
    ti = lax.broadcasted_iota(jnp.int32, (sup, LANES), 0)
    li = lax.broadcasted_iota(jnp.int32, (sup, LANES), 1)
    mi = li - ((ti // c) % (LANES // c)) * c
    own = (mi >= 0) & (mi < c)
    ii = ti % c
    incl = (jnp.where(own & (mi <= ii), 1.0, 0.0), jnp.where(own & (mi >= ii), 1.0, 0.0))
    strict = (jnp.where(own & (mi < ii), 1.0, 0.0), jnp.where(own & (mi > ii), 1.0, 0.0))
    halves = sup // LANES

    def own_tiles(x):
        return jnp.concatenate(x, axis=0)

    def prep(r):
        t0 = pl.multiple_of(r * sup, sup)
        rows = pl.ds(t0, sup)
        q = q_ref[0, rows, :]
        k = k_ref[0, rows, :]
        v = v_ref[0, rows, :]
        grams = []
        for p in range(halves):
            hs = slice(p * LANES, (p + 1) * LANES)
            grams.append(_dot_nt(jnp.concatenate([q[hs], k[hs]], axis=0), k[hs]))
        qk = own_tiles([g[:LANES] for g in grams])
        kk = own_tiles([g[LANES:] for g in grams])
        gates = gcol_ref[rows, :]
        grw = grow_ref[r]
        gcol = lambda j, d: gates[:, 2 * j + d:2 * j + d + 1]
        ls = []
        for d in range(2):
            diff = own_tiles([gcol(0, d)[p * LANES:(p + 1) * LANES] - grw[d:d + 1, p * LANES:(p + 1) * LANES]
                              for p in range(halves)])
            e = jnp.exp(jnp.minimum(diff, 0.0))
            m2_s[d, rows, :] = (qk * (e * incl[d])).astype(BF16)
            lt = kk * (e * strict[d]) * gcol(1, d)
            ls.append(jnp.concatenate(
                [sum(lt[p * LANES + j * c:p * LANES + (j + 1) * c] for j in range(LANES // c))
                 for p in range(halves)], axis=-1))
        return rows, q, k, v, gcol, ls

    def finish(rows, q, k, v, gcol, tbds):
        for d in range(2):
            rhs = jnp.concatenate([v * gcol(1, d), k * gcol(3, d)], axis=-1).astype(BF16)
            sol = jnp.dot(tbds[d], rhs, preferred_element_type=F32)
            u_s[d, rows, :] = sol[:, :HEAD_DIM].astype(BF16)
            w_s[d, rows, :] = sol[:, HEAD_DIM:].astype(BF16)
            qg_s[d, rows, :] = (q * gcol(2, d)).astype(BF16)
            kt_s[d, rows, :] = (k * gcol(4, d)).astype(BF16)

    group = GDN_GROUP if nsup % GDN_GROUP == 0 else 1

    def phase_a(i, carry):
        preps = [prep(i * group + g) for g in range(group)]
        tbds = _unit_tri_inverses([l for pr in preps for l in pr[5]], same, offs, base_s, eye_s)
        for g, pr in enumerate(preps):
            finish(*pr[:5], tbds[2 * g:2 * g + 2])
        return carry

    lax.fori_loop(0, nsup // group, phase_a, 0)

    st_s[...] = jnp.zeros_like(st_s)

    outs = (of_s, ob_s)
    dot = functools.partial(jnp.dot, preferred_element_type=F32)

    def phase_b(i, carry):
        t0s = (pl.multiple_of(i * c, c), pl.multiple_of((n - 1 - i) * c, c))
        rows = [pl.ds(t0, c) for t0 in t0s]
        sts = [st_s[d] for d in range(2)]
        both = [dot(jnp.concatenate([w_s[d, rows[d], :], qg_s[d, rows[d], :]], axis=0),
                    sts[d].astype(BF16)) for d in range(2)]
        vbs = [(u_s[d, rows[d], :].astype(F32) - both[d][:c]).astype(BF16) for d in range(2)]
        for d in range(2):
            vv = jnp.concatenate([vbs[d], vbs[d]], axis=0)
            outs[d][rows[d], :] = both[d][c:] + dot(m2_s[d, rows[d], :], vv)
        for d in range(2):
            decay = gcol_ref[pl.ds(t0s[d], 1), 10 + d:11 + d]
            st_s[d] = sts[d] * decay + _dot_tn(kt_s[d, rows[d], :], vbs[d])
        return carry

    lax.fori_loop(0, n, phase_b, 0, unroll=2)

    nw = nw_ref[...]

    def fin(r, carry):
        sl = pl.ds(pl.multiple_of(r * sup, sup), sup)
        o = of_s[sl, :] + ob_s[sl, :]
        o = o * lax.rsqrt(jnp.mean(o * o, axis=-1, keepdims=True) + NORM_EPS) * nw
        o_ref[0, sl, :] = (o * _silu(z_ref[0, sl, :])).astype(BF16)
        return carry

    lax.fori_loop(0, nsup, fin, 0)


def _gdn(proj, gcol, grow, norm_w):
    b, s, _ = proj.shape
    assert s % GDN_SUPER == 0
    base = 4 * RET_HEADS
    hb = lambda off: pl.BlockSpec((1, s, HEAD_DIM), lambda i, j, off=off: (i, 0, off + j))
    seq2 = pltpu.VMEM((2, s, HEAD_DIM), BF16)
    return pl.pallas_call(
        _gdn_kernel,
        grid=(b, GDN_HEADS),
        in_specs=[hb(base), hb(base + GDN_HEADS), hb(base + 2 * GDN_HEADS), hb(base + 3 * GDN_HEADS),
                  pl.BlockSpec((None, None, s, gcol.shape[-1]), lambda i, j: (i, j, 0, 0)),
                  pl.BlockSpec((None, None, s // GDN_SUPER, 2, GDN_SUPER), lambda i, j: (i, j, 0, 0, 0)),
                  pl.BlockSpec((1, HEAD_DIM), lambda i, j: (0, 0))],
        out_specs=pl.BlockSpec((1, s, HEAD_DIM), lambda i, j: (i, 0, j)),
        out_shape=jax.ShapeDtypeStruct((b, s, GDN_HEADS * HEAD_DIM), BF16),
        scratch_shapes=[seq2, seq2, seq2, seq2, seq2,
                        pltpu.VMEM((s, HEAD_DIM), F32), pltpu.VMEM((s, HEAD_DIM), F32),
                        pltpu.VMEM((2, HEAD_DIM, HEAD_DIM), F32)],
        compiler_params=_cparams("parallel", "parallel"),
        name="gdn",
    )(proj, proj, proj, proj, gcol, grow, norm_w)


def _memkv_kernel(mem_ref, mw_ref, wk_ref, wv_ref, k_ref, v_ref):
    m = _rms(mem_ref[0], mw_ref[...]).astype(BF16)
    k_ref[0] = jnp.dot(m, wk_ref[...], preferred_element_type=F32).astype(BF16)
    v_ref[0] = jnp.dot(m, wv_ref[...], preferred_element_type=F32).astype(BF16)


def _memkv(mem, mw, wk, wv):
    b, m, d = mem.shape
    blk = pl.BlockSpec((1, m, d), lambda i: (i, 0, 0))
    out = jax.ShapeDtypeStruct((b, m, d), BF16)
    return pl.pallas_call(
        _memkv_kernel,
        grid=(b,),
        in_specs=[blk, _resident((1, d)), _resident((d, d)), _resident((d, d))],
        out_specs=[blk, blk],
        out_shape=[out, out],
        compiler_params=_cparams("parallel"),
        name="mem_kv",
    )(mem, mw, wk, wv)


def _xattn_kernel(x_ref, oret_ref, ogdn_ref, wa_ref, wb_ref, npost_mix_ref,
                  npre_ref, wq_ref, k_ref, v_ref, wo_ref, npost_ref, o_ref):
    h_mix = (jnp.dot(oret_ref[0], wa_ref[...], preferred_element_type=F32)
             + jnp.dot(ogdn_ref[0], wb_ref[...], preferred_element_type=F32))
    x = x_ref[0] + _rms(h_mix, npost_mix_ref[...])
    d = x.shape[-1]
    hd = d // XATTN_HEADS
    u = _rms(x, npre_ref[...]).astype(BF16)
    q = jnp.dot(u, wq_ref[...], preferred_element_type=F32).astype(BF16)
    sls = [slice(h * hd, (h + 1) * hd) for h in range(XATTN_HEADS)]
    scs = [lax.dot_general(q[:, sl], k_ref[0, :, sl], (((1,), (1,)), ((), ())),
                           preferred_element_type=F32) * (hd ** -0.5) for sl in sls]
    ps = []
    for sc in scs:
        e = jnp.exp(sc - jnp.max(sc, axis=-1, keepdims=True))
        ps.append((e / jnp.sum(e, axis=-1, keepdims=True)).astype(BF16))
    heads = [jnp.dot(p, v_ref[0, :, sl], preferred_element_type=F32) for p, sl in zip(ps, sls)]
    o = jnp.concatenate(heads, axis=-1).astype(BF16)
    hh = jnp.dot(o, wo_ref[...], preferred_element_type=F32)
    o_ref[0] = x + _rms(hh, npost_ref[...])


def _xattn(x, o_ret, o_gdn, w_a, w_b, npost_mix, npre, wq, kmem, vmem, wo, npost, tm):
    b, s, d = x.shape
    m = kmem.shape[1]
    wr = o_ret.shape[-1]
    wg = o_gdn.shape[-1]
    row = pl.BlockSpec((1, tm, d), lambda i, j: (i, j, 0))
    kv = pl.BlockSpec((1, m, d), lambda i, j: (i, 0, 0))
    return pl.pallas_call(
        _xattn_kernel,
        grid=(b, s // tm),
        in_specs=[row, pl.BlockSpec((1, tm, wr), lambda i, j: (i, j, 0)),
                  pl.BlockSpec((1, tm, wg), lambda i, j: (i, j, 0)),
                  _resident((wr, d)), _resident((wg, d)), _resident((1, d)),
                  _resident((1, d)), _resident((d, d)), kv, kv, _resident((d, d)), _resident((1, d))],
        out_specs=row,
        out_shape=jax.ShapeDtypeStruct((b, s, d), F32),
        compiler_params=_cparams("parallel", "parallel"),
        name="outproj_xattn",
    )(x, o_ret, o_gdn, w_a, w_b, npost_mix, npre, wq, kmem, vmem, wo, npost)


def _pad_lanes(v):
    return jnp.pad(v.reshape(1, -1).astype(F32), ((0, 0), (0, LANES - v.size)))


def kernel(x, mem, positions, norm_pre, norm_post, mem_norm, ffn1_gate, ffn1_up, ffn1_down, w_in, gdn_conv, ret_log_gamma, gdn_a_log, gdn_dt_bias, gdn_norm, w_out, xattn_q, xattn_k, xattn_v, xattn_o, ffn2_gate, ffn2_up, ffn2_down):
    b, s, d = x.shape
    depth = norm_pre.shape[0]
    t = b * s
    tm = 512 if s % 512 == 0 else s
    ret_w = RET_HEADS * HEAD_DIM
    main_cols = 4 * ret_w + 4 * GDN_HEADS * HEAD_DIM
    bf = lambda w: w.astype(BF16)
    vec = lambda w: w.reshape(1, -1).astype(F32)

    cosf, sinf = _rope_tables(positions)
    x2 = x.reshape(t, d)
    for l in range(depth):
        x2 = _ffn(x2, vec(norm_pre[l, 0]), bf(ffn1_gate[l]), bf(ffn1_up[l]), bf(ffn1_down[l]),
                  vec(norm_post[l, 0]), tm)

        wab = jnp.pad(w_in[l][:, main_cols:], ((0, 0), (0, LANES - 4 * GDN_HEADS)))
        wab_hi = wab.astype(BF16)
        wab_lo = (wab - wab_hi.astype(F32)).astype(BF16)
        proj, ab = _inproj(x2, vec(norm_pre[l, 1]), bf(w_in[l][:, :main_cols]), wab_hi, wab_lo,
                           gdn_conv[l].astype(F32), tm, s)
        proj = proj.reshape(b, s, main_cols)
        gates = _gates(ab.reshape(b, s, LANES), _pad_lanes(jnp.exp(gdn_a_log[l].astype(F32))),
                       _pad_lanes(gdn_dt_bias[l]))
        ngrp = 6
        gcol = gates[:, :, :ngrp * 2 * GDN_HEADS].reshape(b, s, ngrp * 2, GDN_HEADS).transpose(0, 3, 1, 2)
        grow = gates[:, :, :2 * GDN_HEADS].reshape(b, s // GDN_SUPER, GDN_SUPER, 2, GDN_HEADS)
        grow = grow.transpose(0, 4, 1, 3, 2)
        o_ret = _retention(proj, cosf, sinf, ret_log_gamma[l].astype(F32))
        o_gdn = _gdn(proj, gcol, grow, vec(gdn_norm[l]))

        kmem, vmem = _memkv(mem, vec(mem_norm[l]), bf(xattn_k[l]), bf(xattn_v[l]))
        x3 = _xattn(x2.reshape(b, s, d), o_ret, o_gdn, bf(w_out[l][:ret_w]), bf(w_out[l][ret_w:]),
                    vec(norm_post[l, 1]), vec(norm_pre[l, 2]), bf(xattn_q[l]), kmem, vmem,
                    bf(xattn_o[l]), vec(norm_post[l, 2]), tm)
        x2 = x3.reshape(t, d)

        x2 = _ffn(x2, vec(norm_pre[l, 3]), bf(ffn2_gate[l]), bf(ffn2_up[l]), bf(ffn2_down[l]),
                  vec(norm_post[l, 3]), tm)
    return x2.reshape(b, s, d)
```

```python
import functools

import jax
import jax.numpy as jnp
from jax import lax
from jax.experimental import pallas as pl
from jax.experimental.pallas import tpu as pltpu

F32 = jnp.float32
BF16 = jnp.bfloat16

HEAD_DIM = 128
RET_HEADS = 4
RET_CHUNK = 128
RET_GROUP = 4
GDN_HEADS = 4
GDN_CHUNK = 64
GDN_SUPER = 256
GDN_BASE = 8
GDN_GROUP = 4
CONV_K = 5
CONV_HALO = 8
XATTN_HEADS = 4
ROPE_BASE = 10000.0
NORM_EPS = 1e-6
LANES = 128
VMEM_LIMIT = 56 * 1024 * 1024


def _cparams(*sem):
    return pltpu.CompilerParams(dimension_semantics=sem, vmem_limit_bytes=VMEM_LIMIT)


def _rms(x, w):
    return x * lax.rsqrt(jnp.mean(x * x, axis=-1, keepdims=True) + NORM_EPS) * w


def _silu(x):
    return x * jax.nn.sigmoid(x)


def _dot(a, b):
    return jnp.dot(a.astype(BF16), b.astype(BF16), preferred_element_type=F32)


def _dot_nt(a, b):
    return lax.dot_general(a.astype(BF16), b.astype(BF16), (((1,), (1,)), ((), ())),
                           preferred_element_type=F32)


def _dot_tn(a, b):
    return lax.dot_general(a.astype(BF16), b.astype(BF16), (((0,), (0,)), ((), ())),
                           preferred_element_type=F32)


def _split2(a):
    hi = a.astype(BF16)
    lo = (a - hi.astype(F32)).astype(BF16)
    return hi, lo


def _split3(a):
    hi = a.astype(BF16)
    r = a - hi.astype(F32)
    mid = r.astype(BF16)
    lo = (r - mid.astype(F32)).astype(BF16)
    return hi, mid, lo


def _rope_kernel(pos_ref, invf_ref, sign_ref, cos_ref, sin_ref):
    ang = pos_ref[0].astype(F32) * invf_ref[...]
    cos_ref[0] = jnp.cos(ang)
    sin_ref[0] = jnp.sin(ang) * sign_ref[...]


def _rope_tables(positions):
    b, s = positions.shape
    ts = min(s, 512)
    half = HEAD_DIM // 2
    inv_freq = ROPE_BASE ** (-jnp.arange(0, HEAD_DIM, 2, dtype=F32) / HEAD_DIM)
    invf = jnp.concatenate([inv_freq, inv_freq])[None, :]
    sign = jnp.concatenate([-jnp.ones((half,), F32), jnp.ones((half,), F32)])[None, :]
    tab = jax.ShapeDtypeStruct((b, s, HEAD_DIM), F32)
    return pl.pallas_call(
        _rope_kernel,
        grid=(b, s // ts),
        in_specs=[pl.BlockSpec((1, ts, 1), lambda i, j: (i, j, 0)),
                  pl.BlockSpec((1, HEAD_DIM), lambda i, j: (0, 0)),
                  pl.BlockSpec((1, HEAD_DIM), lambda i, j: (0, 0))],
        out_specs=[pl.BlockSpec((1, ts, HEAD_DIM), lambda i, j: (i, j, 0)),
                   pl.BlockSpec((1, ts, HEAD_DIM), lambda i, j: (i, j, 0))],
        out_shape=[tab, tab],
        compiler_params=_cparams("parallel", "parallel"),
        name="rope_tables",
    )(positions.reshape(b, s, 1), invf, sign)


def _ffn_kernel(x_ref, npre_ref, wg_ref, wu_ref, wd_ref, npost_ref, o_ref):
    x = x_ref[...]
    u = _rms(x, npre_ref[...]).astype(BF16)
    g = jnp.dot(u, wg_ref[...], preferred_element_type=F32)
    up = jnp.dot(u, wu_ref[...], preferred_element_type=F32)
    h = (_silu(g) * up).astype(BF16)
    y = jnp.dot(h, wd_ref[...], preferred_element_type=F32)
    o_ref[...] = x + 0.5 * _rms(y, npost_ref[...])


def _resident(shape):
    return pl.BlockSpec(shape, lambda *_: (0,) * len(shape), pipeline_mode=pl.Buffered(1))


def _ffn(x2, npre, wg, wu, wd, npost, tm):
    t, d = x2.shape
    f = wg.shape[1]
    row = pl.BlockSpec((tm, d), lambda i: (i, 0))
    return pl.pallas_call(
        _ffn_kernel,
        grid=(t // tm,),
        in_specs=[row, _resident((1, d)), _resident((d, f)), _resident((d, f)),
                  _resident((f, d)), _resident((1, d))],
        out_specs=row,
        out_shape=jax.ShapeDtypeStruct((t, d), F32),
        compiler_params=_cparams("parallel"),
        name="ffn",
    )(x2, npre, wg, wu, wd, npost)


def _inproj_kernel(x_ref, xprev_ref, xnext_ref, npre_ref, w_ref, wab_hi_ref, wab_lo_ref, cw_ref,
                   proj_ref, ab_ref, ext_s, *, tiles_per_seq):
    tm = x_ref.shape[0]
    halo = CONV_HALO
    left = (CONV_K - 1) // 2
    q0 = 4 * RET_HEADS * HEAD_DIM
    qw = 3 * GDN_HEADS * HEAD_DIM
    npre = npre_ref[...]
    u = _rms(x_ref[...], npre)
    u_hi, u_lo = _split2(u)
    d = functools.partial(jnp.dot, preferred_element_type=F32)

    i = pl.program_id(0) % tiles_per_seq
    u_ext = jnp.concatenate([_rms(xprev_ref[...], npre).astype(BF16), u_hi,
                             _rms(xnext_ref[...], npre).astype(BF16)], axis=0)
    ext = d(u_ext, w_ref[:, q0:q0 + qw])
    nheads = qw // HEAD_DIM
    zeros = jnp.zeros((halo, HEAD_DIM), F32)
    for hidx in range(nheads):
        cs = slice(hidx * HEAD_DIM, (hidx + 1) * HEAD_DIM)
        ext_s[hidx, 0:halo, :] = jnp.where(i == 0, zeros, ext[0:halo, cs])
        ext_s[hidx, halo:halo + tm, :] = ext[halo:halo + tm, cs]
        ext_s[hidx, halo + tm:, :] = jnp.where(i == tiles_per_seq - 1, zeros, ext[halo + tm:, cs])

    ab_ref[...] = d(u_hi, wab_hi_ref[...]) + (d(u_lo, wab_hi_ref[...]) + d(u_hi, wab_lo_ref[...]))

    tap0 = jnp.minimum(pl.program_id(0), 0) + (halo - left)
    wide = 2 * LANES
    other = list(range(0, q0, wide)) + list(range(q0 + qw, proj_ref.shape[1], wide))
    for col in range(0, qw, HEAD_DIM):
        hidx = col // HEAD_DIM
        todo = other[hidx:hidx + 1] if hidx + 1 < nheads else other[hidx:]
        for oc in todo:
            proj_ref[:, oc:oc + wide] = d(u_hi, w_ref[:, oc:oc + wide])
        cs = slice(col, col + HEAD_DIM)
        w = cw_ref[:, cs]
        acc = ext_s[hidx, pl.ds(tap0, tm), :] * w[0:1, :]
        for j in range(1, CONV_K):
            acc = acc + ext_s[hidx, pl.ds(tap0 + j, tm), :] * w[j:j + 1, :]
        y = _silu(acc)
        if col < 2 * GDN_HEADS * HEAD_DIM:
            y = y * lax.rsqrt(jnp.sum(y * y, axis=-1, keepdims=True) + NORM_EPS)
            if col < GDN_HEADS * HEAD_DIM:
                y = y * (HEAD_DIM ** -0.5)
        proj_ref[:, q0 + col:q0 + col + HEAD_DIM] = y


def _inproj(x2, npre, w_main, wab_hi, wab_lo, conv_w, tm, s):
    t, d = x2.shape
    n = w_main.shape[1]
    hb = tm // CONV_HALO
    last = t // CONV_HALO - 1
    return pl.pallas_call(
        functools.partial(_inproj_kernel, tiles_per_seq=s // tm),
        grid=(t // tm,),
        in_specs=[pl.BlockSpec((tm, d), lambda i: (i, 0)),
                  pl.BlockSpec((CONV_HALO, d), lambda i: (jnp.maximum(i * hb - 1, 0), 0)),
                  pl.BlockSpec((CONV_HALO, d), lambda i: (jnp.minimum((i + 1) * hb, last), 0)),
                  _resident((1, d)), _resident((d, n)), _resident((d, LANES)), _resident((d, LANES)),
                  _resident(conv_w.shape)],
        out_specs=[pl.BlockSpec((tm, n), lambda i: (i, 0)),
                   pl.BlockSpec((tm, LANES), lambda i: (i, 0))],
        out_shape=[jax.ShapeDtypeStruct((t, n), F32), jax.ShapeDtypeStruct((t, LANES), F32)],
        scratch_shapes=[pltpu.VMEM((3 * GDN_HEADS, tm + 2 * CONV_HALO, HEAD_DIM), F32)],
        compiler_params=_cparams("parallel"),
        name="inproj",
    )(x2, x2, x2, npre, w_main, wab_hi, wab_lo, conv_w)


def _gate_kernel(ab_ref, aexp_ref, dtb_ref, o_ref):
    s = ab_ref.shape[2]
    sup = GDN_SUPER
    grp = 2 * GDN_HEADS
    ri = lax.broadcasted_iota(jnp.int32, (sup, sup), 0)
    ci = lax.broadcasted_iota(jnp.int32, (sup, sup), 1)
    same = (ri // GDN_CHUNK) == (ci // GDN_CHUNK)
    upto = jnp.where(same & (ri <= ci), 1.0, 0.0).astype(BF16)
    from_ = jnp.where(same & (ri >= ci), 1.0, 0.0).astype(BF16)
    row = lax.broadcasted_iota(jnp.int32, (2 * grp, sup), 0)
    aexp = aexp_ref[...]
    dtb = dtb_ref[...]
    d = functools.partial(jnp.dot, preferred_element_type=F32)

    def body(r, carry):
        cols = pl.ds(pl.multiple_of(r * sup, sup), sup)
        ab = ab_ref[0, :, cols]
        xs = ab + dtb
        softplus = jnp.maximum(xs, 0.0) + jnp.log1p(jnp.exp(-jnp.abs(xs)))
        g = -aexp * softplus
        beta = jax.nn.sigmoid(ab)
        g_hi, g_mid, g_lo = _split3(g)
        pre = d(g_hi, upto) + (d(g_mid, upto) + d(g_lo, upto))
        suf = d(g_hi, from_) + (d(g_mid, from_) + d(g_lo, from_))
        tot = pre + (suf - g)
        gc = jnp.where(row < GDN_HEADS, pre, suf)
        eg = jnp.exp(gc)
        o_ref[0, :, cols] = jnp.concatenate(
            [gc[:grp], beta[grp:], eg[:grp], beta[grp:] * eg[:grp],
             jnp.exp(tot - gc)[:grp], jnp.exp(tot)[:grp]], axis=0)
        return carry

    lax.fori_loop(0, s // sup, body, 0)


def _gates(ab_t, aexp_col, dtb_col):
    b, rows, s = ab_t.shape
    vec = pl.BlockSpec((rows, 1), lambda i: (0, 0))
    return pl.pallas_call(
        _gate_kernel,
        grid=(b,),
        in_specs=[pl.BlockSpec((1, rows, s), lambda i: (i, 0, 0)), vec, vec],
        out_specs=pl.BlockSpec((1, 3 * rows, s), lambda i: (i, 0, 0)),
        out_shape=jax.ShapeDtypeStruct((b, 3 * rows, s), F32),
        compiler_params=_cparams("parallel"),
        name="gdn_gates",
    )(ab_t, aexp_col, dtb_col)


def _retention_kernel(lgam_ref, rq_ref, rk_ref, rv_ref, rg_ref, cos_ref, sin_ref, o_ref,
                      q_s, k_s, kvf_s, stb_s, st_s):
    s = rq_ref.shape[1]
    c = RET_CHUNK
    n = s // c
    grp = RET_GROUP if n % RET_GROUP == 0 else 1
    h = pl.program_id(1)
    lg_f = lgam_ref[0, h]
    lg_b = lgam_ref[1, h]

    ri = lax.broadcasted_iota(jnp.int32, (c, c), 0)
    ci = lax.broadcasted_iota(jnp.int32, (c, c), 1)
    rel = (ri - ci).astype(F32)
    m_f = rel >= 0
    dmat = jnp.where(m_f, jnp.exp(jnp.where(m_f, rel, 0.0) * lg_f),
                     jnp.exp(jnp.where(m_f, 0.0, -rel) * lg_b))
    pos = lax.broadcasted_iota(jnp.int32, (c, HEAD_DIM), 0).astype(F32)
    qdec_f = jnp.exp((pos + 1.0) * lg_f)
    kdec_f = jnp.exp((c - 1.0 - pos) * lg_f)
    qdec_b = jnp.exp((c - pos) * lg_b)
    kdec_b = jnp.exp(pos * lg_b)
    cd_f = jnp.exp(c * lg_f)
    cd_b = jnp.exp(c * lg_b)

    def rows_of(chunk):
        return pl.ds(pl.multiple_of(chunk * c, c), c)

    st_s[...] = jnp.zeros_like(st_s)

    def pass1(i, carry):
        chunks = [n - 1 - (i * grp + g) for g in range(grp)]
        kvs = []
        for ch in chunks:
            sl = rows_of(ch)
            cs = cos_ref[0, sl, :]
            sn = sin_ref[0, sl, :]
            q = rq_ref[0, sl, :]
            k = rk_ref[0, sl, :]
            v = rv_ref[0, sl, :]
            q_s[sl, :] = (q * cs + pltpu.roll(q, HEAD_DIM // 2, 1) * sn).astype(BF16)
            kr = ((k * cs + pltpu.roll(k, HEAD_DIM // 2, 1) * sn) * (HEAD_DIM ** -0.5)).astype(BF16)
            k_s[sl, :] = kr
            kvs.append(_dot_tn(kr, jnp.concatenate([v * kdec_f, v * kdec_b], axis=-1)))
        st = st_s[...]
        for ch, kv in zip(chunks, kvs):
            kvf_s[ch] = kv[:, :HEAD_DIM]
            stb_s[ch] = st.astype(BF16)
            st = st * cd_b + kv[:, HEAD_DIM:]
        st_s[...] = st
        return carry

    lax.fori_loop(0, n // grp, pass1, 0)

    st_s[...] = jnp.zeros_like(st_s)

    def pass2(i, carry):
        chunks = [i * grp + g for g in range(grp)]
        st = st_s[...]
        sts = []
        for ch in chunks:
            sts.append(jnp.concatenate([st.astype(BF16), stb_s[ch]], axis=-1))
            st = st * cd_f + kvf_s[ch]
        st_s[...] = st
        qs = [q_s[rows_of(ch), :] for ch in chunks]
        scores = [(_dot_nt(q, k_s[rows_of(ch), :]) * dmat).astype(BF16) for q, ch in zip(qs, chunks)]
        inter = [jnp.dot(q, st2, preferred_element_type=F32) for q, st2 in zip(qs, sts)]
        for ch, sc, it in zip(chunks, scores, inter):
            sl = rows_of(ch)
            o = (jnp.dot(sc, rv_ref[0, sl, :].astype(BF16), preferred_element_type=F32)
                 + it[:, :HEAD_DIM] * qdec_f + it[:, HEAD_DIM:] * qdec_b)
            mu = jnp.mean(o, axis=-1, keepdims=True)
            oc = o - mu
            var = jnp.mean(oc * oc, axis=-1, keepdims=True)
            o_ref[0, sl, :] = (_silu(rg_ref[0, sl, :]) * (oc * lax.rsqrt(var + NORM_EPS))).astype(BF16)
        return carry

    lax.fori_loop(0, n // grp, pass2, 0)


def _retention(proj, cosf, sinf, lgam):
    b, s, _ = proj.shape
    hb = lambda off: pl.BlockSpec((1, s, HEAD_DIM), lambda i, j, off=off: (i, 0, off + j))
    tab = pl.BlockSpec((1, s, HEAD_DIM), lambda i, j: (i, 0, 0))
    return pl.pallas_call(
        _retention_kernel,
        grid=(b, RET_HEADS),
        in_specs=[pl.BlockSpec(memory_space=pltpu.SMEM),
                  hb(0), hb(RET_HEADS), hb(2 * RET_HEADS), hb(3 * RET_HEADS), tab, tab],
        out_specs=pl.BlockSpec((1, s, HEAD_DIM), lambda i, j: (i, 0, j)),
        out_shape=jax.ShapeDtypeStruct((b, s, RET_HEADS * HEAD_DIM), BF16),
        scratch_shapes=[pltpu.VMEM((s, HEAD_DIM), BF16), pltpu.VMEM((s, HEAD_DIM), BF16),
                        pltpu.VMEM((s // RET_CHUNK, HEAD_DIM, HEAD_DIM), F32),
                        pltpu.VMEM((s // RET_CHUNK, HEAD_DIM, HEAD_DIM), BF16),
                        pltpu.VMEM((HEAD_DIM, HEAD_DIM), F32)],
        compiler_params=_cparams("parallel", "parallel"),
        name="retention",
    )(lgam, proj, proj, proj, proj, cosf, sinf)


def _tile_rows(xs, reps):
    return jnp.concatenate([xs.astype(BF16)] * reps, axis=0)


def _unit_tri_inverses(ls, same, offs, base_s, eye_s):
    reps = same.shape[0] // eye_s.shape[0]
    dot = functools.partial(jnp.dot, preferred_element_type=F32)
    bd = lambda x: _tile_rows(x, reps) * same
    tiled = [_tile_rows(l, reps) for l in ls]
    c = eye_s.shape[0]
    xs = [l * base_s for l in ls]
    ps = [eye_s - x for x in xs]
    levels = (GDN_BASE - 1).bit_length() - 1
    xs = [dot(x.astype(BF16), bd(x)) for x in xs]
    for level in range(levels):
        ws = [bd(x) for x in xs]
        if level + 1 < levels:
            both = [dot(jnp.concatenate([p, x], axis=0).astype(BF16), w) for p, x, w in zip(ps, xs, ws)]
            ps = [p + b[:c] for p, b in zip(ps, both)]
            xs = [b[c:] for b in both]
        else:
            ps = [p + dot(p.astype(BF16), w) for p, w in zip(ps, ws)]
    for off in offs:
        ys = [dot(p.astype(BF16), t * off) for p, t in zip(ps, tiled)]
        ws = [bd(p) for p in ps]
        ps = [p - dot(y.astype(BF16), w) for p, y, w in zip(ps, ys, ws)]
    return [bd(p) for p in ps]


def _gdn_kernel(q_ref, k_ref, v_ref, z_ref, gcol_ref, grow_ref, nw_ref,
                o_ref, u_s, w_s, qg_s, kt_s, m2_s, of_s, ob_s, st_s):
    s = q_ref.shape[1]
    c = GDN_CHUNK
    n = s // c
    sup = GDN_SUPER
    nsup = s // sup

    ri = lax.broadcasted_iota(jnp.int32, (sup, sup), 0)
    ci = lax.broadcasted_iota(jnp.int32, (sup, sup), 1)
    blk = lambda size: (ri // size) == (ci // size)
    same = jnp.where(blk(c), 1.0, 0.0).astype(BF16)
    offs = []
    size = GDN_BASE
    while size < c:
        offs.append(jnp.where(blk(2 * size) & ~blk(size), 1.0, 0.0).astype(BF16))
        size *= 2
    si = lax.broadcasted_iota(jnp.int32, (c, sup), 0)
    sm = lax.broadcasted_iota(jnp.int32, (c, sup), 1) % c
    eye_s = jnp.where(si == sm, 1.0, 0.0)
    base_s = jnp.where(si // GDN_BASE == sm // GDN_BASE, 1.0, 0.0)
    ti = lax.broadcasted_iota(jnp.int32, (sup, LANES), 0)
    li = lax.broadcasted_iota(jnp.int32, (sup, LANES), 1)
    mi = li - ((ti // c) % (LANES // c)) * c
    own = (mi >= 0) & (mi < c)
    ii = ti % c
    incl = (jnp.where(own & (mi <= ii), 1.0, 0.0), jnp.where(own & (mi >= ii), 1.0, 0.0))
    strict = (jnp.where(own & (mi < ii), 1.0, 0.0), jnp.where(own & (mi > ii), 1.0, 0.0))
    halves = sup // LANES

    def own_tiles(x):
        return jnp.concatenate(x, axis=0)

    def prep(r):
        t0 = pl.multiple_of(r * sup, sup)
        rows = pl.ds(t0, sup)
        q = q_ref[0, rows, :]
        k = k_ref[0, rows, :]
        v = v_ref[0, rows, :]
        grams = []
        for p in range(halves):
            hs = slice(p * LANES, (p + 1) * LANES)
            grams.append(_dot_nt(jnp.concatenate([q[hs], k[hs]], axis=0), k[hs]))
        qk = own_tiles([g[:LANES] for g in grams])
        kk = own_tiles([g[LANES:] for g in grams])
        gates = gcol_ref[rows, :]
        grw = grow_ref[r]
        gcol = lambda j, d: gates[:, 2 * j + d:2 * j + d + 1]
        ls = []
        for d in range(2):
            diff = own_tiles([gcol(0, d)[p * LANES:(p + 1) * LANES] - grw[d:d + 1, p * LANES:(p + 1) * LANES]
                              for p in range(halves)])
            e = jnp.exp(jnp.minimum(diff, 0.0))
            m2_s[d, rows, :] = (qk * (e * incl[d])).astype(BF16)
            lt = kk * (e * strict[d]) * gcol(1, d)
            ls.append(jnp.concatenate(
                [sum(lt[p * LANES + j * c:p * LANES + (j + 1) * c] for j in range(LANES // c))
                 for p in range(halves)], axis=-1))
        return rows, q, k, v, gcol, ls

    def finish(rows, q, k, v, gcol, tbds):
        for d in range(2):
            rhs = jnp.concatenate([v * gcol(1, d), k * gcol(3, d)], axis=-1).astype(BF16)
            sol = jnp.dot(tbds[d], rhs, preferred_element_type=F32)
            u_s[d, rows, :] = sol[:, :HEAD_DIM].astype(BF16)
            w_s[d, rows, :] = sol[:, HEAD_DIM:].astype(BF16)
            qg_s[d, rows, :] = (q * gcol(2, d)).astype(BF16)
            kt_s[d, rows, :] = (k * gcol(4, d)).astype(BF16)

    group = GDN_GROUP if nsup % GDN_GROUP == 0 else 1

    def phase_a(i, carry):
        preps = [prep(i * group + g) for g in range(group)]
        tbds = _unit_tri_inverses([l for pr in preps for l in pr[5]], same, offs, base_s, eye_s)
        for g, pr in enumerate(preps):
            finish(*pr[:5], tbds[2 * g:2 * g + 2])
        return carry

    lax.fori_loop(0, nsup // group, phase_a, 0)

    st_s[...] = jnp.zeros_like(st_s)

    outs = (of_s, ob_s)
    dot = functools.partial(jnp.dot, preferred_element_type=F32)

    def phase_b(i, carry):
        t0s = (pl.multiple_of(i * c, c), pl.multiple_of((n - 1 - i) * c, c))
        rows = [pl.ds(t0, c) for t0 in t0s]
        sts = [st_s[d] for d in range(2)]
        both = [dot(jnp.concatenate([w_s[d, rows[d], :], qg_s[d, rows[d], :]], axis=0),
                    sts[d].astype(BF16)) for d in range(2)]
        vbs = [(u_s[d, rows[d], :].astype(F32) - both[d][:c]).astype(BF16) for d in range(2)]
        for d in range(2):
            vv = jnp.concatenate([vbs[d], vbs[d]], axis=0)
            outs[d][rows[d], :] = both[d][c:] + dot(m2_s[d, rows[d], :], vv)
        for d in range(2):
            decay = gcol_ref[pl.ds(t0s[d], 1), 10 + d:11 + d]
            st_s[d] = sts[d] * decay + _dot_tn(kt_s[d, rows[d], :], vbs[d])
        return carry

    lax.fori_loop(0, n, phase_b, 0, unroll=2)

    nw = nw_ref[...]

    def fin(r, carry):
        sl = pl.ds(pl.multiple_of(r * sup, sup), sup)
        o = of_s[sl, :] + ob_s[sl, :]
        o = o * lax.rsqrt(jnp.mean(o * o, axis=-1, keepdims=True) + NORM_EPS) * nw
        o_ref[0, sl, :] = (o * _silu(z_ref[0, sl, :])).astype(BF16)
        return carry

    lax.fori_loop(0, nsup, fin, 0)


def _gdn(proj, gcol, grow, norm_w):
    b, s, _ = proj.shape
    assert s % GDN_SUPER == 0
    base = 4 * RET_HEADS
    hb = lambda off: pl.BlockSpec((1, s, HEAD_DIM), lambda i, j, off=off: (i, 0, off + j))
    seq2 = pltpu.VMEM((2, s, HEAD_DIM), BF16)
    return pl.pallas_call(
        _gdn_kernel,
        grid=(b, GDN_HEADS),
        in_specs=[hb(base), hb(base + GDN_HEADS), hb(base + 2 * GDN_HEADS), hb(base + 3 * GDN_HEADS),
                  pl.BlockSpec((None, None, s, gcol.shape[-1]), lambda i, j: (i, j, 0, 0)),
                  pl.BlockSpec((None, None, s // GDN_SUPER, 2, GDN_SUPER), lambda i, j: (i, j, 0, 0, 0)),
                  pl.BlockSpec((1, HEAD_DIM), lambda i, j: (0, 0))],
        out_specs=pl.BlockSpec((1, s, HEAD_DIM), lambda i, j: (i, 0, j)),
        out_shape=jax.ShapeDtypeStruct((b, s, GDN_HEADS * HEAD_DIM), BF16),
        scratch_shapes=[seq2, seq2, seq2, seq2, seq2,
                        pltpu.VMEM((s, HEAD_DIM), F32), pltpu.VMEM((s, HEAD_DIM), F32),
                        pltpu.VMEM((2, HEAD_DIM, HEAD_DIM), F32)],
        compiler_params=_cparams("parallel", "parallel"),
        name="gdn",
    )(proj, proj, proj, proj, gcol, grow, norm_w)


def _memkv_kernel(mem_ref, mw_ref, wk_ref, wv_ref, k_ref, v_ref):
    m = _rms(mem_ref[0], mw_ref[...]).astype(BF16)
    k_ref[0] = jnp.dot(m, wk_ref[...], preferred_element_type=F32).astype(BF16)
    v_ref[0] = jnp.dot(m, wv_ref[...], preferred_element_type=F32).astype(BF16)


def _memkv(mem, mw, wk, wv):
    b, m, d = mem.shape
    blk = pl.BlockSpec((1, m, d), lambda i: (i, 0, 0))
    out = jax.ShapeDtypeStruct((b, m, d), BF16)
    return pl.pallas_call(
        _memkv_kernel,
        grid=(b,),
        in_specs=[blk, _resident((1, d)), _resident((d, d)), _resident((d, d))],
        out_specs=[blk, blk],
        out_shape=[out, out],
        compiler_params=_cparams("parallel"),
        name="mem_kv",
    )(mem, mw, wk, wv)


def _xattn_kernel(x_ref, oret_ref, ogdn_ref, wa_ref, wb_ref, npost_mix_ref,
                  npre_ref, wq_ref, k_ref, v_ref, wo_ref, npost_ref, o_ref):
    h_mix = (jnp.dot(oret_ref[0], wa_ref[...], preferred_element_type=F32)
             + jnp.dot(ogdn_ref[0], wb_ref[...], preferred_element_type=F32))
    x = x_ref[0] + _rms(h_mix, npost_mix_ref[...])
    d = x.shape[-1]
    hd = d // XATTN_HEADS
    u = _rms(x, npre_ref[...]).astype(BF16)
    q = jnp.dot(u, wq_ref[...], preferred_element_type=F32).astype(BF16)
    sls = [slice(h * hd, (h + 1) * hd) for h in range(XATTN_HEADS)]
    scs = [lax.dot_general(q[:, sl], k_ref[0, :, sl], (((1,), (1,)), ((), ())),
                           preferred_element_type=F32) * (hd ** -0.5) for sl in sls]
    ps = []
    for sc in scs:
        e = jnp.exp(sc - jnp.max(sc, axis=-1, keepdims=True))
        ps.append((e / jnp.sum(e, axis=-1, keepdims=True)).astype(BF16))
    heads = [jnp.dot(p, v_ref[0, :, sl], preferred_element_type=F32) for p, sl in zip(ps, sls)]
    o = jnp.concatenate(heads, axis=-1).astype(BF16)
    hh = jnp.dot(o, wo_ref[...], preferred_element_type=F32)
    o_ref[0] = x + _rms(hh, npost_ref[...])


def _xattn(x, o_ret, o_gdn, w_a, w_b, npost_mix, npre, wq, kmem, vmem, wo, npost, tm):
    b, s, d = x.shape
    m = kmem.shape[1]
    wr = o_ret.shape[-1]
    wg = o_gdn.shape[-1]
    row = pl.BlockSpec((1, tm, d), lambda i, j: (i, j, 0))
    kv = pl.BlockSpec((1, m, d), lambda i, j: (i, 0, 0))
    return pl.pallas_call(
        _xattn_kernel,
        grid=(b, s // tm),
        in_specs=[row, pl.BlockSpec((1, tm, wr), lambda i, j: (i, j, 0)),
                  pl.BlockSpec((1, tm, wg), lambda i, j: (i, j, 0)),
                  _resident((wr, d)), _resident((wg, d)), _resident((1, d)),
                  _resident((1, d)), _resident((d, d)), kv, kv, _resident((d, d)), _resident((1, d))],
        out_specs=row,
        out_shape=jax.ShapeDtypeStruct((b, s, d), F32),
        compiler_params=_cparams("parallel", "parallel"),
        name="outproj_xattn",
    )(x, o_ret, o_gdn, w_a, w_b, npost_mix, npre, wq, kmem, vmem, wo, npost)


def _pad_rows(v, rows):
    return jnp.pad(v.reshape(-1, 1).astype(F32), ((0, rows - v.size), (0, 0)))


def kernel(x, mem, positions, norm_pre, norm_post, mem_norm, ffn1_gate, ffn1_up, ffn1_down, w_in, gdn_conv, ret_log_gamma, gdn_a_log, gdn_dt_bias, gdn_norm, w_out, xattn_q, xattn_k, xattn_v, xattn_o, ffn2_gate, ffn2_up, ffn2_down):
    b, s, d = x.shape
    depth = norm_pre.shape[0]
    t = b * s
    tm = 512 if s % 512 == 0 else s
    ret_w = RET_HEADS * HEAD_DIM
    main_cols = 4 * ret_w + 4 * GDN_HEADS * HEAD_DIM
    bf = lambda w: w.astype(BF16)
    vec = lambda w: w.reshape(1, -1).astype(F32)

    cosf, sinf = _rope_tables(positions)
    x2 = x.reshape(t, d)
    for l in range(depth):
        x2 = _ffn(x2, vec(norm_pre[l, 0]), bf(ffn1_gate[l]), bf(ffn1_up[l]), bf(ffn1_down[l]),
                  vec(norm_post[l, 0]), tm)

        wab = jnp.pad(w_in[l][:, main_cols:], ((0, 0), (0, LANES - 4 * GDN_HEADS)))
        wab_hi = wab.astype(BF16)
        wab_lo = (wab - wab_hi.astype(F32)).astype(BF16)
        proj, ab = _inproj(x2, vec(norm_pre[l, 1]), bf(w_in[l][:, :main_cols]), wab_hi, wab_lo,
                           gdn_conv[l].astype(F32), tm, s)
        proj = proj.reshape(b, s, main_cols)
        nlog = 4 * GDN_HEADS
        ab_t = ab.reshape(b, s, LANES)[:, :, :nlog].transpose(0, 2, 1)
        gates = _gates(ab_t, _pad_rows(jnp.exp(gdn_a_log[l].astype(F32)), nlog),
                       _pad_rows(gdn_dt_bias[l], nlog))
        gcol = gates.reshape(b, -1, 2, GDN_HEADS, s).transpose(0, 3, 4, 1, 2).reshape(b, GDN_HEADS, s, -1)
        grow = gates[:, :2 * GDN_HEADS].reshape(b, 2, GDN_HEADS, s // GDN_SUPER, GDN_SUPER)
        grow = grow.transpose(0, 2, 3, 1, 4)
        o_ret = _retention(proj, cosf, sinf, ret_log_gamma[l].astype(F32))
        o_gdn = _gdn(proj, gcol, grow, vec(gdn_norm[l]))

        kmem, vmem = _memkv(mem, vec(mem_norm[l]), bf(xattn_k[l]), bf(xattn_v[l]))
        x3 = _xattn(x2.reshape(b, s, d), o_ret, o_gdn, bf(w_out[l][:ret_w]), bf(w_out[l][ret_w:]),
                    vec(norm_post[l, 1]), vec(norm_pre[l, 2]), bf(xattn_q[l]), kmem, vmem,
                    bf(xattn_o[l]), vec(norm_post[l, 2]), tm)
        x2 = x3.reshape(t, d)

        x2 = _ffn(x2, vec(norm_pre[l, 3]), bf(ffn2_gate[l]), bf(ffn2_up[l]), bf(ffn2_down[l]),
                  vec(norm_post[l, 3]), tm)
    return x2.reshape(b, s, d)
```

```python
import functools

import jax
import jax.numpy as jnp
from jax import lax
from jax.experimental import pallas as pl
from jax.experimental.pallas import tpu as pltpu

F32 = jnp.float32
BF16 = jnp.bfloat16

HEAD_DIM = 128
RET_HEADS = 4
RET_CHUNK = 128
RET_GROUP = 4
GDN_HEADS = 4
GDN_CHUNK = 128
GDN_SUPER = 256
GDN_BASE = 8
GDN_GROUP = 4
CONV_K = 5
CONV_HALO = 8
XATTN_HEADS = 4
ROPE_BASE = 10000.0
NORM_EPS = 1e-6
LANES = 128
VMEM_LIMIT = 56 * 1024 * 1024


def _cparams(*sem):
    return pltpu.CompilerParams(dimension_semantics=sem, vmem_limit_bytes=VMEM_LIMIT)


def _rms(x, w):
    return x * lax.rsqrt(jnp.mean(x * x, axis=-1, keepdims=True) + NORM_EPS) * w


def _silu(x):
    return x * jax.nn.sigmoid(x)


def _dot(a, b):
    return jnp.dot(a.astype(BF16), b.astype(BF16), preferred_element_type=F32)


def _dot_nt(a, b):
    return lax.dot_general(a.astype(BF16), b.astype(BF16), (((1,), (1,)), ((), ())),
                           preferred_element_type=F32)


def _dot_tn(a, b):
    return lax.dot_general(a.astype(BF16), b.astype(BF16), (((0,), (0,)), ((), ())),
                           preferred_element_type=F32)


def _split2(a):
    hi = a.astype(BF16)
    lo = (a - hi.astype(F32)).astype(BF16)
    return hi, lo


def _split3(a):
    hi = a.astype(BF16)
    r = a - hi.astype(F32)
    mid = r.astype(BF16)
    lo = (r - mid.astype(F32)).astype(BF16)
    return hi, mid, lo


def _rope_kernel(pos_ref, invf_ref, sign_ref, cos_ref, sin_ref):
    ang = pos_ref[0].astype(F32) * invf_ref[...]
    cos_ref[0] = jnp.cos(ang)
    sin_ref[0] = jnp.sin(ang) * sign_ref[...]


def _rope_tables(positions):
    b, s = positions.shape
    ts = min(s, 512)
    half = HEAD_DIM // 2
    inv_freq = ROPE_BASE ** (-jnp.arange(0, HEAD_DIM, 2, dtype=F32) / HEAD_DIM)
    invf = jnp.concatenate([inv_freq, inv_freq])[None, :]
    sign = jnp.concatenate([-jnp.ones((half,), F32), jnp.ones((half,), F32)])[None, :]
    tab = jax.ShapeDtypeStruct((b, s, HEAD_DIM), F32)
    return pl.pallas_call(
        _rope_kernel,
        grid=(b, s // ts),
        in_specs=[pl.BlockSpec((1, ts, 1), lambda i, j: (i, j, 0)),
                  pl.BlockSpec((1, HEAD_DIM), lambda i, j: (0, 0)),
                  pl.BlockSpec((1, HEAD_DIM), lambda i, j: (0, 0))],
        out_specs=[pl.BlockSpec((1, ts, HEAD_DIM), lambda i, j: (i, j, 0)),
                   pl.BlockSpec((1, ts, HEAD_DIM), lambda i, j: (i, j, 0))],
        out_shape=[tab, tab],
        compiler_params=_cparams("parallel", "parallel"),
        name="rope_tables",
    )(positions.reshape(b, s, 1), invf, sign)


def _ffn_kernel(x_ref, npre_ref, wg_ref, wu_ref, wd_ref, npost_ref, o_ref):
    x = x_ref[...]
    u = _rms(x, npre_ref[...]).astype(BF16)
    g = jnp.dot(u, wg_ref[...], preferred_element_type=F32)
    up = jnp.dot(u, wu_ref[...], preferred_element_type=F32)
    h = (_silu(g) * up).astype(BF16)
    y = jnp.dot(h, wd_ref[...], preferred_element_type=F32)
    o_ref[...] = x + 0.5 * _rms(y, npost_ref[...])


def _resident(shape):
    return pl.BlockSpec(shape, lambda *_: (0,) * len(shape), pipeline_mode=pl.Buffered(1))


def _ffn(x2, npre, wg, wu, wd, npost, tm):
    t, d = x2.shape
    f = wg.shape[1]
    row = pl.BlockSpec((tm, d), lambda i: (i, 0))
    return pl.pallas_call(
        _ffn_kernel,
        grid=(t // tm,),
        in_specs=[row, _resident((1, d)), _resident((d, f)), _resident((d, f)),
                  _resident((f, d)), _resident((1, d))],
        out_specs=row,
        out_shape=jax.ShapeDtypeStruct((t, d), F32),
        compiler_params=_cparams("parallel"),
        name="ffn",
    )(x2, npre, wg, wu, wd, npost)


def _inproj_kernel(x_ref, xprev_ref, xnext_ref, npre_ref, w_ref, wab_hi_ref, wab_lo_ref, cw_ref,
                   proj_ref, ab_ref, ext_s, *, tiles_per_seq):
    tm = x_ref.shape[0]
    halo = CONV_HALO
    left = (CONV_K - 1) // 2
    q0 = 4 * RET_HEADS * HEAD_DIM
    qw = 3 * GDN_HEADS * HEAD_DIM
    npre = npre_ref[...]
    u = _rms(x_ref[...], npre)
    u_hi, u_lo = _split2(u)
    d = functools.partial(jnp.dot, preferred_element_type=F32)

    i = pl.program_id(0) % tiles_per_seq
    u_ext = jnp.concatenate([_rms(xprev_ref[...], npre).astype(BF16), u_hi,
                             _rms(xnext_ref[...], npre).astype(BF16)], axis=0)
    ext = d(u_ext, w_ref[:, q0:q0 + qw])
    nheads = qw // HEAD_DIM
    zeros = jnp.zeros((halo, HEAD_DIM), F32)
    for hidx in range(nheads):
        cs = slice(hidx * HEAD_DIM, (hidx + 1) * HEAD_DIM)
        ext_s[hidx, 0:halo, :] = jnp.where(i == 0, zeros, ext[0:halo, cs])
        ext_s[hidx, halo:halo + tm, :] = ext[halo:halo + tm, cs]
        ext_s[hidx, halo + tm:, :] = jnp.where(i == tiles_per_seq - 1, zeros, ext[halo + tm:, cs])

    ab_ref[...] = d(u_hi, wab_hi_ref[...]) + (d(u_lo, wab_hi_ref[...]) + d(u_hi, wab_lo_ref[...]))

    tap0 = jnp.minimum(pl.program_id(0), 0) + (halo - left)
    wide = 2 * LANES
    other = list(range(0, q0, wide)) + list(range(q0 + qw, proj_ref.shape[1], wide))
    for col in range(0, qw, HEAD_DIM):
        hidx = col // HEAD_DIM
        todo = other[hidx:hidx + 1] if hidx + 1 < nheads else other[hidx:]
        for oc in todo:
            proj_ref[:, oc:oc + wide] = d(u_hi, w_ref[:, oc:oc + wide])
        cs = slice(col, col + HEAD_DIM)
        w = cw_ref[:, cs]
        acc = ext_s[hidx, pl.ds(tap0, tm), :] * w[0:1, :]
        for j in range(1, CONV_K):
            acc = acc + ext_s[hidx, pl.ds(tap0 + j, tm), :] * w[j:j + 1, :]
        y = _silu(acc)
        if col < 2 * GDN_HEADS * HEAD_DIM:
            y = y * lax.rsqrt(jnp.sum(y * y, axis=-1, keepdims=True) + NORM_EPS)
            if col < GDN_HEADS * HEAD_DIM:
                y = y * (HEAD_DIM ** -0.5)
        proj_ref[:, q0 + col:q0 + col + HEAD_DIM] = y


def _inproj(x2, npre, w_main, wab_hi, wab_lo, conv_w, tm, s):
    t, d = x2.shape
    n = w_main.shape[1]
    hb = tm // CONV_HALO
    last = t // CONV_HALO - 1
    return pl.pallas_call(
        functools.partial(_inproj_kernel, tiles_per_seq=s // tm),
        grid=(t // tm,),
        in_specs=[pl.BlockSpec((tm, d), lambda i: (i, 0)),
                  pl.BlockSpec((CONV_HALO, d), lambda i: (jnp.maximum(i * hb - 1, 0), 0)),
                  pl.BlockSpec((CONV_HALO, d), lambda i: (jnp.minimum((i + 1) * hb, last), 0)),
                  _resident((1, d)), _resident((d, n)), _resident((d, LANES)), _resident((d, LANES)),
                  _resident(conv_w.shape)],
        out_specs=[pl.BlockSpec((tm, n), lambda i: (i, 0)),
                   pl.BlockSpec((tm, LANES), lambda i: (i, 0))],
        out_shape=[jax.ShapeDtypeStruct((t, n), F32), jax.ShapeDtypeStruct((t, LANES), F32)],
        scratch_shapes=[pltpu.VMEM((3 * GDN_HEADS, tm + 2 * CONV_HALO, HEAD_DIM), F32)],
        compiler_params=_cparams("parallel"),
        name="inproj",
    )(x2, x2, x2, npre, w_main, wab_hi, wab_lo, conv_w)


def _gate_kernel(ab_ref, aexp_ref, dtb_ref, o_ref):
    s = ab_ref.shape[2]
    sup = GDN_SUPER
    grp = 2 * GDN_HEADS
    ri = lax.broadcasted_iota(jnp.int32, (sup, sup), 0)
    ci = lax.broadcasted_iota(jnp.int32, (sup, sup), 1)
    same = (ri // GDN_CHUNK) == (ci // GDN_CHUNK)
    upto = jnp.where(same & (ri <= ci), 1.0, 0.0).astype(BF16)
    from_ = jnp.where(same & (ri >= ci), 1.0, 0.0).astype(BF16)
    row = lax.broadcasted_iota(jnp.int32, (2 * grp, sup), 0)
    aexp = aexp_ref[...]
    dtb = dtb_ref[...]
    d = functools.partial(jnp.dot, preferred_element_type=F32)

    def body(r, carry):
        cols = pl.ds(pl.multiple_of(r * sup, sup), sup)
        ab = ab_ref[0, :, cols]
        xs = ab + dtb
        softplus = jnp.maximum(xs, 0.0) + jnp.log1p(jnp.exp(-jnp.abs(xs)))
        g = -aexp * softplus
        beta = jax.nn.sigmoid(ab)
        g_hi, g_mid, g_lo = _split3(g)
        pre = d(g_hi, upto) + (d(g_mid, upto) + d(g_lo, upto))
        suf = d(g_hi, from_) + (d(g_mid, from_) + d(g_lo, from_))
        tot = pre + (suf - g)
        gc = jnp.where(row < GDN_HEADS, pre, suf)
        eg = jnp.exp(gc)
        o_ref[0, :, cols] = jnp.concatenate(
            [gc[:grp], beta[grp:], eg[:grp], beta[grp:] * eg[:grp],
             jnp.exp(tot - gc)[:grp], jnp.exp(tot)[:grp]], axis=0)
        return carry

    lax.fori_loop(0, s // sup, body, 0)


def _gates(ab_t, aexp_col, dtb_col):
    b, rows, s = ab_t.shape
    vec = pl.BlockSpec((rows, 1), lambda i: (0, 0))
    return pl.pallas_call(
        _gate_kernel,
        grid=(b,),
        in_specs=[pl.BlockSpec((1, rows, s), lambda i: (i, 0, 0)), vec, vec],
        out_specs=pl.BlockSpec((1, 3 * rows, s), lambda i: (i, 0, 0)),
        out_shape=jax.ShapeDtypeStruct((b, 3 * rows, s), F32),
        compiler_params=_cparams("parallel"),
        name="gdn_gates",
    )(ab_t, aexp_col, dtb_col)


def _retention_kernel(lgam_ref, rq_ref, rk_ref, rv_ref, rg_ref, cos_ref, sin_ref, o_ref,
                      q_s, k_s, kvf_s, stb_s, st_s):
    s = rq_ref.shape[1]
    c = RET_CHUNK
    n = s // c
    grp = RET_GROUP if n % RET_GROUP == 0 else 1
    h = pl.program_id(1)
    lg_f = lgam_ref[0, h]
    lg_b = lgam_ref[1, h]

    ri = lax.broadcasted_iota(jnp.int32, (c, c), 0)
    ci = lax.broadcasted_iota(jnp.int32, (c, c), 1)
    rel = (ri - ci).astype(F32)
    m_f = rel >= 0
    dmat = jnp.where(m_f, jnp.exp(jnp.where(m_f, rel, 0.0) * lg_f),
                     jnp.exp(jnp.where(m_f, 0.0, -rel) * lg_b))
    pos = lax.broadcasted_iota(jnp.int32, (c, HEAD_DIM), 0).astype(F32)
    qdec_f = jnp.exp((pos + 1.0) * lg_f)
    kdec_f = jnp.exp((c - 1.0 - pos) * lg_f)
    qdec_b = jnp.exp((c - pos) * lg_b)
    kdec_b = jnp.exp(pos * lg_b)
    cd_f = jnp.exp(c * lg_f)
    cd_b = jnp.exp(c * lg_b)

    def rows_of(chunk):
        return pl.ds(pl.multiple_of(chunk * c, c), c)

    st_s[...] = jnp.zeros_like(st_s)

    def pass1(i, carry):
        chunks = [n - 1 - (i * grp + g) for g in range(grp)]
        kvs = []
        for ch in chunks:
            sl = rows_of(ch)
            cs = cos_ref[0, sl, :]
            sn = sin_ref[0, sl, :]
            q = rq_ref[0, sl, :]
            k = rk_ref[0, sl, :]
            v = rv_ref[0, sl, :]
            q_s[sl, :] = (q * cs + pltpu.roll(q, HEAD_DIM // 2, 1) * sn).astype(BF16)
            kr = ((k * cs + pltpu.roll(k, HEAD_DIM // 2, 1) * sn) * (HEAD_DIM ** -0.5)).astype(BF16)
            k_s[sl, :] = kr
            kvs.append(_dot_tn(kr, jnp.concatenate([v * kdec_f, v * kdec_b], axis=-1)))
        st = st_s[...]
        for ch, kv in zip(chunks, kvs):
            kvf_s[ch] = kv[:, :HEAD_DIM]
            stb_s[ch] = st.astype(BF16)
            st = st * cd_b + kv[:, HEAD_DIM:]
        st_s[...] = st
        return carry

    lax.fori_loop(0, n // grp, pass1, 0)

    st_s[...] = jnp.zeros_like(st_s)

    def pass2(i, carry):
        chunks = [i * grp + g for g in range(grp)]
        st = st_s[...]
        sts = []
        for ch in chunks:
            sts.append(jnp.concatenate([st.astype(BF16), stb_s[ch]], axis=-1))
            st = st * cd_f + kvf_s[ch]
        st_s[...] = st
        qs = [q_s[rows_of(ch), :] for ch in chunks]
        scores = [(_dot_nt(q, k_s[rows_of(ch), :]) * dmat).astype(BF16) for q, ch in zip(qs, chunks)]
        inter = [jnp.dot(q, st2, preferred_element_type=F32) for q, st2 in zip(qs, sts)]
        for ch, sc, it in zip(chunks, scores, inter):
            sl = rows_of(ch)
            o = (jnp.dot(sc, rv_ref[0, sl, :].astype(BF16), preferred_element_type=F32)
                 + it[:, :HEAD_DIM] * qdec_f + it[:, HEAD_DIM:] * qdec_b)
            mu = jnp.mean(o, axis=-1, keepdims=True)
            oc = o - mu
            var = jnp.mean(oc * oc, axis=-1, keepdims=True)
            o_ref[0, sl, :] = (_silu(rg_ref[0, sl, :]) * (oc * lax.rsqrt(var + NORM_EPS))).astype(BF16)
        return carry

    lax.fori_loop(0, n // grp, pass2, 0)


def _retention(proj, cosf, sinf, lgam):
    b, s, _ = proj.shape
    hb = lambda off: pl.BlockSpec((1, s, HEAD_DIM), lambda i, j, off=off: (i, 0, off + j))
    tab = pl.BlockSpec((1, s, HEAD_DIM), lambda i, j: (i, 0, 0))
    return pl.pallas_call(
        _retention_kernel,
        grid=(b, RET_HEADS),
        in_specs=[pl.BlockSpec(memory_space=pltpu.SMEM),
                  hb(0), hb(RET_HEADS), hb(2 * RET_HEADS), hb(3 * RET_HEADS), tab, tab],
        out_specs=pl.BlockSpec((1, s, HEAD_DIM), lambda i, j: (i, 0, j)),
        out_shape=jax.ShapeDtypeStruct((b, s, RET_HEADS * HEAD_DIM), BF16),
        scratch_shapes=[pltpu.VMEM((s, HEAD_DIM), BF16), pltpu.VMEM((s, HEAD_DIM), BF16),
                        pltpu.VMEM((s // RET_CHUNK, HEAD_DIM, HEAD_DIM), F32),
                        pltpu.VMEM((s // RET_CHUNK, HEAD_DIM, HEAD_DIM), BF16),
                        pltpu.VMEM((HEAD_DIM, HEAD_DIM), F32)],
        compiler_params=_cparams("parallel", "parallel"),
        name="retention",
    )(lgam, proj, proj, proj, proj, cosf, sinf)


def _tile_rows(xs, reps):
    return jnp.concatenate([xs.astype(BF16)] * reps, axis=0)


def _unit_tri_inverses(ls, same, offs, base_s, eye_s):
    reps = same.shape[0] // eye_s.shape[0]
    dot = functools.partial(jnp.dot, preferred_element_type=F32)
    bd = lambda x: _tile_rows(x, reps) * same
    tiled = [_tile_rows(l, reps) for l in ls]
    c = eye_s.shape[0]
    xs = [l * base_s for l in ls]
    ps = [eye_s - x for x in xs]
    levels = (GDN_BASE - 1).bit_length() - 1
    xs = [dot(x.astype(BF16), bd(x)) for x in xs]
    for level in range(levels):
        ws = [bd(x) for x in xs]
        if level + 1 < levels:
            both = [dot(jnp.concatenate([p, x], axis=0).astype(BF16), w) for p, x, w in zip(ps, xs, ws)]
            ps = [p + b[:c] for p, b in zip(ps, both)]
            xs = [b[c:] for b in both]
        else:
            ps = [p + dot(p.astype(BF16), w) for p, w in zip(ps, ws)]
    for off in offs:
        ys = [dot(p.astype(BF16), t * off) for p, t in zip(ps, tiled)]
        ws = [bd(p) for p in ps]
        ps = [p - dot(y.astype(BF16), w) for p, y, w in zip(ps, ys, ws)]
    return [bd(p) for p in ps]


def _gdn_kernel(q_ref, k_ref, v_ref, z_ref, gcol_ref, grow_ref, nw_ref,
                o_ref, u_s, w_s, qg_s, kt_s, m2_s, of_s, ob_s, st_s):
    s = q_ref.shape[1]
    c = GDN_CHUNK
    n = s // c
    sup = GDN_SUPER
    nsup = s // sup

    ri = lax.broadcasted_iota(jnp.int32, (sup, sup), 0)
    ci = lax.broadcasted_iota(jnp.int32, (sup, sup), 1)
    blk = lambda size: (ri // size) == (ci // size)
    same = jnp.where(blk(c), 1.0, 0.0).astype(BF16)
    offs = []
    size = GDN_BASE
    while size < c:
        offs.append(jnp.where(blk(2 * size) & ~blk(size), 1.0, 0.0).astype(BF16))
        size *= 2
    si = lax.broadcasted_iota(jnp.int32, (c, sup), 0)
    sm = lax.broadcasted_iota(jnp.int32, (c, sup), 1) % c
    eye_s = jnp.where(si == sm, 1.0, 0.0)
    base_s = jnp.where(si // GDN_BASE == sm // GDN_BASE, 1.0, 0.0)
    ti = lax.broadcasted_iota(jnp.int32, (sup, LANES), 0)
    li = lax.broadcasted_iota(jnp.int32, (sup, LANES), 1)
    mi = li - ((ti // c) % (LANES // c)) * c
    own = (mi >= 0) & (mi < c)
    ii = ti % c
    incl = (jnp.where(own & (mi <= ii), 1.0, 0.0), jnp.where(own & (mi >= ii), 1.0, 0.0))
    strict = (jnp.where(own & (mi < ii), 1.0, 0.0), jnp.where(own & (mi > ii), 1.0, 0.0))
    halves = sup // LANES

    def own_tiles(x):
        return jnp.concatenate(x, axis=0)

    def prep(r):
        t0 = pl.multiple_of(r * sup, sup)
        rows = pl.ds(t0, sup)
        q = q_ref[0, rows, :]
        k = k_ref[0, rows, :]
        v = v_ref[0, rows, :]
        grams = []
        for p in range(halves):
            hs = slice(p * LANES, (p + 1) * LANES)
            grams.append(_dot_nt(jnp.concatenate([q[hs], k[hs]], axis=0), k[hs]))
        qk = own_tiles([g[:LANES] for g in grams])
        kk = own_tiles([g[LANES:] for g in grams])
        gates = gcol_ref[rows, :]
        grw = grow_ref[r]
        gcol = lambda j, d: gates[:, 2 * j + d:2 * j + d + 1]
        ls = []
        for d in range(2):
            diff = own_tiles([gcol(0, d)[p * LANES:(p + 1) * LANES] - grw[d:d + 1, p * LANES:(p + 1) * LANES]
                              for p in range(halves)])
            e = jnp.exp(jnp.minimum(diff, 0.0))
            m2_s[d, rows, :] = (qk * (e * incl[d])).astype(BF16)
            lt = kk * (e * strict[d]) * gcol(1, d)
            ls.append(jnp.concatenate(
                [sum(lt[p * LANES + j * c:p * LANES + (j + 1) * c] for j in range(LANES // c))
                 for p in range(halves)], axis=-1))
        return rows, q, k, v, gcol, ls

    def finish(rows, q, k, v, gcol, tbds):
        for d in range(2):
            rhs = jnp.concatenate([v * gcol(1, d), k * gcol(3, d)], axis=-1).astype(BF16)
            sol = jnp.dot(tbds[d], rhs, preferred_element_type=F32)
            u_s[d, rows, :] = sol[:, :HEAD_DIM].astype(BF16)
            w_s[d, rows, :] = sol[:, HEAD_DIM:].astype(BF16)
            qg_s[d, rows, :] = (q * gcol(2, d)).astype(BF16)
            kt_s[d, rows, :] = (k * gcol(4, d)).astype(BF16)

    group = GDN_GROUP if nsup % GDN_GROUP == 0 else 1

    def phase_a(i, carry):
        preps = [prep(i * group + g) for g in range(group)]
        tbds = _unit_tri_inverses([l for pr in preps for l in pr[5]], same, offs, base_s, eye_s)
        for g, pr in enumerate(preps):
            finish(*pr[:5], tbds[2 * g:2 * g + 2])
        return carry

    lax.fori_loop(0, nsup // group, phase_a, 0)

    st_s[...] = jnp.zeros_like(st_s)

    outs = (of_s, ob_s)
    dot = functools.partial(jnp.dot, preferred_element_type=F32)

    def phase_b(i, carry):
        t0s = (pl.multiple_of(i * c, c), pl.multiple_of((n - 1 - i) * c, c))
        rows = [pl.ds(t0, c) for t0 in t0s]
        sts = [st_s[d] for d in range(2)]
        both = [dot(jnp.concatenate([w_s[d, rows[d], :], qg_s[d, rows[d], :]], axis=0),
                    sts[d].astype(BF16)) for d in range(2)]
        vbs = [(u_s[d, rows[d], :].astype(F32) - both[d][:c]).astype(BF16) for d in range(2)]
        for d in range(2):
            vv = jnp.concatenate([vbs[d]] * (LANES // c), axis=0)
            outs[d][rows[d], :] = both[d][c:] + dot(m2_s[d, rows[d], :], vv)
        for d in range(2):
            decay = gcol_ref[pl.ds(t0s[d], 1), 10 + d:11 + d]
            st_s[d] = sts[d] * decay + _dot_tn(kt_s[d, rows[d], :], vbs[d])
        return carry

    lax.fori_loop(0, n, phase_b, 0, unroll=2)

    nw = nw_ref[...]

    def fin(r, carry):
        sl = pl.ds(pl.multiple_of(r * sup, sup), sup)
        o = of_s[sl, :] + ob_s[sl, :]
        o = o * lax.rsqrt(jnp.mean(o * o, axis=-1, keepdims=True) + NORM_EPS) * nw
        o_ref[0, sl, :] = (o * _silu(z_ref[0, sl, :])).astype(BF16)
        return carry

    lax.fori_loop(0, nsup, fin, 0)


def _gdn(proj, gcol, grow, norm_w):
    b, s, _ = proj.shape
    assert s % GDN_SUPER == 0
    base = 4 * RET_HEADS
    hb = lambda off: pl.BlockSpec((1, s, HEAD_DIM), lambda i, j, off=off: (i, 0, off + j))
    seq2 = pltpu.VMEM((2, s, HEAD_DIM), BF16)
    return pl.pallas_call(
        _gdn_kernel,
        grid=(b, GDN_HEADS),
        in_specs=[hb(base), hb(base + GDN_HEADS), hb(base + 2 * GDN_HEADS), hb(base + 3 * GDN_HEADS),
                  pl.BlockSpec((None, None, s, gcol.shape[-1]), lambda i, j: (i, j, 0, 0)),
                  pl.BlockSpec((None, None, s // GDN_SUPER, 2, GDN_SUPER), lambda i, j: (i, j, 0, 0, 0)),
                  pl.BlockSpec((1, HEAD_DIM), lambda i, j: (0, 0))],
        out_specs=pl.BlockSpec((1, s, HEAD_DIM), lambda i, j: (i, 0, j)),
        out_shape=jax.ShapeDtypeStruct((b, s, GDN_HEADS * HEAD_DIM), BF16),
        scratch_shapes=[seq2, seq2, seq2, seq2, seq2,
                        pltpu.VMEM((s, HEAD_DIM), F32), pltpu.VMEM((s, HEAD_DIM), F32),
                        pltpu.VMEM((2, HEAD_DIM, HEAD_DIM), F32)],
        compiler_params=_cparams("parallel", "parallel"),
        name="gdn",
    )(proj, proj, proj, proj, gcol, grow, norm_w)


def _memkv_kernel(mem_ref, mw_ref, wk_ref, wv_ref, k_ref, v_ref):
    m = _rms(mem_ref[0], mw_ref[...]).astype(BF16)
    k_ref[0] = jnp.dot(m, wk_ref[...], preferred_element_type=F32).astype(BF16)
    v_ref[0] = jnp.dot(m, wv_ref[...], preferred_element_type=F32).astype(BF16)


def _memkv(mem, mw, wk, wv):
    b, m, d = mem.shape
    blk = pl.BlockSpec((1, m, d), lambda i: (i, 0, 0))
    out = jax.ShapeDtypeStruct((b, m, d), BF16)
    return pl.pallas_call(
        _memkv_kernel,
        grid=(b,),
        in_specs=[blk, _resident((1, d)), _resident((d, d)), _resident((d, d))],
        out_specs=[blk, blk],
        out_shape=[out, out],
        compiler_params=_cparams("parallel"),
        name="mem_kv",
    )(mem, mw, wk, wv)


def _xattn_kernel(x_ref, oret_ref, ogdn_ref, wa_ref, wb_ref, npost_mix_ref,
                  npre_ref, wq_ref, k_ref, v_ref, wo_ref, npost_ref, o_ref):
    h_mix = (jnp.dot(oret_ref[0], wa_ref[...], preferred_element_type=F32)
             + jnp.dot(ogdn_ref[0], wb_ref[...], preferred_element_type=F32))
    x = x_ref[0] + _rms(h_mix, npost_mix_ref[...])
    d = x.shape[-1]
    hd = d // XATTN_HEADS
    u = _rms(x, npre_ref[...]).astype(BF16)
    q = jnp.dot(u, wq_ref[...], preferred_element_type=F32).astype(BF16)
    sls = [slice(h * hd, (h + 1) * hd) for h in range(XATTN_HEADS)]
    scs = [lax.dot_general(q[:, sl], k_ref[0, :, sl], (((1,), (1,)), ((), ())),
                           preferred_element_type=F32) * (hd ** -0.5) for sl in sls]
    ps = []
    for sc in scs:
        e = jnp.exp(sc - jnp.max(sc, axis=-1, keepdims=True))
        ps.append((e / jnp.sum(e, axis=-1, keepdims=True)).astype(BF16))
    heads = [jnp.dot(p, v_ref[0, :, sl], preferred_element_type=F32) for p, sl in zip(ps, sls)]
    o = jnp.concatenate(heads, axis=-1).astype(BF16)
    hh = jnp.dot(o, wo_ref[...], preferred_element_type=F32)
    o_ref[0] = x + _rms(hh, npost_ref[...])


def _xattn(x, o_ret, o_gdn, w_a, w_b, npost_mix, npre, wq, kmem, vmem, wo, npost, tm):
    b, s, d = x.shape
    m = kmem.shape[1]
    wr = o_ret.shape[-1]
    wg = o_gdn.shape[-1]
    row = pl.BlockSpec((1, tm, d), lambda i, j: (i, j, 0))
    kv = pl.BlockSpec((1, m, d), lambda i, j: (i, 0, 0))
    return pl.pallas_call(
        _xattn_kernel,
        grid=(b, s // tm),
        in_specs=[row, pl.BlockSpec((1, tm, wr), lambda i, j: (i, j, 0)),
                  pl.BlockSpec((1, tm, wg), lambda i, j: (i, j, 0)),
                  _resident((wr, d)), _resident((wg, d)), _resident((1, d)),
                  _resident((1, d)), _resident((d, d)), kv, kv, _resident((d, d)), _resident((1, d))],
        out_specs=row,
        out_shape=jax.ShapeDtypeStruct((b, s, d), F32),
        compiler_params=_cparams("parallel", "parallel"),
        name="outproj_xattn",
    )(x, o_ret, o_gdn, w_a, w_b, npost_mix, npre, wq, kmem, vmem, wo, npost)


def _pad_rows(v, rows):
    return jnp.pad(v.reshape(-1, 1).astype(F32), ((0, rows - v.size), (0, 0)))


def kernel(x, mem, positions, norm_pre, norm_post, mem_norm, ffn1_gate, ffn1_up, ffn1_down, w_in, gdn_conv, ret_log_gamma, gdn_a_log, gdn_dt_bias, gdn_norm, w_out, xattn_q, xattn_k, xattn_v, xattn_o, ffn2_gate, ffn2_up, ffn2_down):
    b, s, d = x.shape
    depth = norm_pre.shape[0]
    t = b * s
    tm = 512 if s % 512 == 0 else s
    ret_w = RET_HEADS * HEAD_DIM
    main_cols = 4 * ret_w + 4 * GDN_HEADS * HEAD_DIM
    bf = lambda w: w.astype(BF16)
    vec = lambda w: w.reshape(1, -1).astype(F32)

    cosf, sinf = _rope_tables(positions)
    x2 = x.reshape(t, d)
    for l in range(depth):
        x2 = _ffn(x2, vec(norm_pre[l, 0]), bf(ffn1_gate[l]), bf(ffn1_up[l]), bf(ffn1_down[l]),
                  vec(norm_post[l, 0]), tm)

        wab = jnp.pad(w_in[l][:, main_cols:], ((0, 0), (0, LANES - 4 * GDN_HEADS)))
        wab_hi = wab.astype(BF16)
        wab_lo = (wab - wab_hi.astype(F32)).astype(BF16)
        proj, ab = _inproj(x2, vec(norm_pre[l, 1]), bf(w_in[l][:, :main_cols]), wab_hi, wab_lo,
                           gdn_conv[l].astype(F32), tm, s)
        proj = proj.reshape(b, s, main_cols)
        nlog = 4 * GDN_HEADS
        ab_t = ab.reshape(b, s, LANES)[:, :, :nlog].transpose(0, 2, 1)
        gates = _gates(ab_t, _pad_rows(jnp.exp(gdn_a_log[l].astype(F32)), nlog),
                       _pad_rows(gdn_dt_bias[l], nlog))
        gcol = gates.reshape(b, -1, 2, GDN_HEADS, s).transpose(0, 3, 4, 1, 2).reshape(b, GDN_HEADS, s, -1)
        grow = gates[:, :2 * GDN_HEADS].reshape(b, 2, GDN_HEADS, s // GDN_SUPER, GDN_SUPER)
        grow = grow.transpose(0, 2, 3, 1, 4)
        o_ret = _retention(proj, cosf, sinf, ret_log_gamma[l].astype(F32))
        o_gdn = _gdn(proj, gcol, grow, vec(gdn_norm[l]))

        kmem, vmem = _memkv(mem, vec(mem_norm[l]), bf(xattn_k[l]), bf(xattn_v[l]))
        x3 = _xattn(x2.reshape(b, s, d), o_ret, o_gdn, bf(w_out[l][:ret_w]), bf(w_out[l][ret_w:]),
                    vec(norm_post[l, 1]), vec(norm_pre[l, 2]), bf(xattn_q[l]), kmem, vmem,
                    bf(xattn_o[l]), vec(norm_post[l, 2]), tm)
        x2 = x3.reshape(t, d)

        x2 = _ffn(x2, vec(norm_pre[l, 3]), bf(ffn2_gate[l]), bf(ffn2_up[l]), bf(ffn2_down[l]),
                  vec(norm_post[l, 3]), tm)
    return x2.reshape(b, s, d)
```
